```python
import jax, jax.numpy as jnp
from jax import lax
import numpy as np

D_MODEL = 1024
BATCH = 2
SEQ = 8192
DEPTH = 1

MIX_WIDTH = D_MODEL
POOL_WIDTH = MIX_WIDTH // 2
CONV_WIDTH = MIX_WIDTH - POOL_WIDTH
POOL_WINDOWS = (2, 4, 8, 16)
N_POOL_GROUPS = len(POOL_WINDOWS)
POOL_GROUP_DIM = POOL_WIDTH // N_POOL_GROUPS
CONV_HEAD_DIM = 64
N_CONV_HEADS = CONV_WIDTH // CONV_HEAD_DIM
CONV_K = 3
IN_COLS = POOL_WIDTH + 3 * CONV_WIDTH
D_FF = ((8 * D_MODEL // 3 + 255) // 256) * 256
RMS_EPS = 1e-6

kernel_name = "hybrid_pool_shortconv_block"


def _rmsnorm(x, g):
    xf = x.astype(jnp.float32)
    inv = lax.rsqrt(jnp.mean(xf * xf, axis=-1, keepdims=True) + RMS_EPS)
    return (xf * inv).astype(x.dtype) * g


def _trailing_pool_minus_self(u, window):
    seq = u.shape[1]
    uf = u.astype(jnp.float32)
    cs = jnp.cumsum(uf, axis=1)
    cs_lag = jnp.pad(cs, ((0, 0), (window, 0), (0, 0)))[:, :seq]
    cnt = jnp.minimum(jnp.arange(1, seq + 1), window).astype(jnp.float32)
    return ((cs - cs_lag) / cnt[None, :, None] - uf).astype(u.dtype)


def _pool_mixer(v, pool_w, pool_scale):
    b, s, _ = v.shape
    vg = v.reshape(b, s, N_POOL_GROUPS, POOL_GROUP_DIM)
    pooled = jnp.stack(
        [_trailing_pool_minus_self(vg[:, :, i], w) for i, w in enumerate(POOL_WINDOWS)],
        axis=2)
    mixed = jnp.einsum('bsgc,gcd->bsgd', pooled, pool_w)
    return mixed.reshape(b, s, POOL_WIDTH) * pool_scale


def _causal_depthwise_conv(u, conv_w):
    s = u.shape[1]
    up = jnp.pad(u, ((0, 0), (CONV_K - 1, 0), (0, 0)))
    return sum(up[:, k:k + s] * conv_w[k] for k in range(CONV_K))


def _conv_mixer(gb, gc, h, conv_w):
    return gb * _causal_depthwise_conv(gc * h, conv_w)


def setup_inputs(seed: int = 0) -> dict:
    key = jax.random.key(seed)
    ks = jax.random.split(key, 12)
    f32 = jnp.float32
    x = jax.random.normal(ks[0], (BATCH, SEQ, D_MODEL), f32)
    norm1_g = 1.0 + 0.05 * jax.random.normal(ks[1], (D_MODEL,), f32)
    w_in = jax.random.normal(ks[2], (D_MODEL, IN_COLS), f32) * D_MODEL ** -0.5
    pool_w = jax.random.normal(ks[3], (N_POOL_GROUPS, POOL_GROUP_DIM, POOL_GROUP_DIM), f32) * POOL_GROUP_DIM ** -0.5
    pool_scale = 1.0 + 0.05 * jax.random.normal(ks[4], (POOL_WIDTH,), f32)
    conv_w = jax.random.normal(ks[5], (CONV_K, CONV_WIDTH), f32) * CONV_K ** -0.5
    w_out = jax.random.normal(ks[6], (MIX_WIDTH, D_MODEL), f32) * MIX_WIDTH ** -0.5
    norm2_g = 1.0 + 0.05 * jax.random.normal(ks[7], (D_MODEL,), f32)
    w_gate = jax.random.normal(ks[8], (D_MODEL, D_FF), f32) * D_MODEL ** -0.5
    w_up = jax.random.normal(ks[9], (D_MODEL, D_FF), f32) * D_MODEL ** -0.5
    w_down = jax.random.normal(ks[10], (D_FF, D_MODEL), f32) * D_FF ** -0.5
    normf_g = 1.0 + 0.05 * jax.random.normal(ks[11], (D_MODEL,), f32)
    return {"x": x, "norm1_g": norm1_g, "w_in": w_in, "pool_w": pool_w,
            "pool_scale": pool_scale, "conv_w": conv_w, "w_out": w_out,
            "norm2_g": norm2_g, "w_gate": w_gate, "w_up": w_up,
            "w_down": w_down, "normf_g": normf_g}


def reference(x, norm1_g, w_in, pool_w, pool_scale, conv_w, w_out,
              norm2_g, w_gate, w_up, w_down, normf_g):
    for _ in range(DEPTH):
        hn = _rmsnorm(x, norm1_g)
        proj = jnp.einsum('bsd,dc->bsc', hn, w_in)
        v_pool = proj[..., :POOL_WIDTH]
        gb = proj[..., POOL_WIDTH:POOL_WIDTH + CONV_WIDTH]
        gc = proj[..., POOL_WIDTH + CONV_WIDTH:POOL_WIDTH + 2 * CONV_WIDTH]
        h = proj[..., POOL_WIDTH + 2 * CONV_WIDTH:]
        y_pool = _pool_mixer(v_pool, pool_w, pool_scale)
        y_conv = _conv_mixer(gb, gc, h, conv_w)
        y_mix = jnp.concatenate([y_pool, y_conv], axis=-1)
        x = x + jnp.einsum('bsc,cd->bsd', y_mix, w_out)
        hn2 = _rmsnorm(x, norm2_g)
        g = jnp.einsum('bsd,df->bsf', hn2, w_gate)
        u = jnp.einsum('bsd,df->bsf', hn2, w_up)
        x = x + jnp.einsum('bsf,fd->bsd', jax.nn.silu(g) * u, w_down)
    return _rmsnorm(x, normf_g)
```

```python
import functools

import jax
import jax.numpy as jnp
from jax import lax
from jax.experimental import pallas as pl
from jax.experimental.pallas import tpu as pltpu

D_MODEL = 1024
POOL_WIDTH = 512
CONV_WIDTH = 512
POOL_WINDOWS = (2, 4, 8, 16)
POOL_GROUP_DIM = 128
CONV_K = 3
IN_COLS = POOL_WIDTH + 3 * CONV_WIDTH
D_FF = 2816
RMS_EPS = 1e-6

TM = 512
POOL_HALO = 16
CONV_HALO = 8
FF_CHUNK = 256
VMEM_LIMIT_BYTES = 56 * 1024 * 1024


def _rmsnorm(x, g):
    ms = jnp.mean(x * x, axis=-1, keepdims=True)
    return (x * lax.rsqrt(ms + RMS_EPS)) * g


def _block_kernel(tiles_per_seq,
                  x_ref, g1_ref, w_in_ref, pool_w_ref, pool_scale_ref, conv_w_ref,
                  w_out_ref, g2_ref, w_gate_ref, w_up_ref, w_down_ref, gf_ref,
                  o_ref, vbuf, ubuf, ymix, acc):
    i = pl.program_id(0)

    @pl.when(i % tiles_per_seq == 0)
    def _():
        vbuf[0:POOL_HALO, :] = jnp.zeros((POOL_HALO, POOL_WIDTH), jnp.float32)
        ubuf[0:CONV_HALO, :] = jnp.zeros((CONV_HALO, CONV_WIDTH), jnp.float32)

    x = x_ref[...]
    hn = _rmsnorm(x, g1_ref[...]).astype(jnp.bfloat16)
    proj = jnp.dot(hn, w_in_ref[...], preferred_element_type=jnp.float32)

    vbuf[POOL_HALO:POOL_HALO + TM, :] = proj[:, :POOL_WIDTH]
    row = i % tiles_per_seq * TM + lax.broadcasted_iota(jnp.int32, (TM, 1), 0)
    for g, w in enumerate(POOL_WINDOWS):
        cols = slice(g * POOL_GROUP_DIM, (g + 1) * POOL_GROUP_DIM)
        cur = vbuf[POOL_HALO:POOL_HALO + TM, cols]
        s = cur
        for j in range(1, w):
            s = s + vbuf[POOL_HALO - j:POOL_HALO - j + TM, cols]
        cnt = jnp.minimum(row + 1, w).astype(jnp.float32)
        pooled = (s / cnt - cur).astype(jnp.bfloat16)
        mixed = jnp.dot(pooled, pool_w_ref[g], preferred_element_type=jnp.float32)
        ymix[:, cols] = (mixed * pool_scale_ref[:, cols]).astype(jnp.bfloat16)

    gb = proj[:, POOL_WIDTH:POOL_WIDTH + CONV_WIDTH]
    gc = proj[:, POOL_WIDTH + CONV_WIDTH:POOL_WIDTH + 2 * CONV_WIDTH]
    h = proj[:, POOL_WIDTH + 2 * CONV_WIDTH:]
    ubuf[CONV_HALO:CONV_HALO + TM, :] = gc * h
    conv = ubuf[CONV_HALO:CONV_HALO + TM, :] * conv_w_ref[CONV_K - 1:CONV_K, :]
    for k in range(CONV_K - 1):
        lag = CONV_K - 1 - k
        conv = conv + ubuf[CONV_HALO - lag:CONV_HALO - lag + TM, :] * conv_w_ref[k:k + 1, :]
    ymix[:, POOL_WIDTH:] = (gb * conv).astype(jnp.bfloat16)

    vbuf[0:POOL_HALO, :] = vbuf[TM:TM + POOL_HALO, :]
    ubuf[0:CONV_HALO, :] = ubuf[TM:TM + CONV_HALO, :]

    x1 = x + jnp.dot(ymix[...], w_out_ref[...], preferred_element_type=jnp.float32)

    hn2 = _rmsnorm(x1, g2_ref[...]).astype(jnp.bfloat16)
    acc[...] = x1
    for c in range(D_FF // FF_CHUNK):
        cols = slice(c * FF_CHUNK, (c + 1) * FF_CHUNK)
        gate = jnp.dot(hn2, w_gate_ref[:, cols], preferred_element_type=jnp.float32)
        up = jnp.dot(hn2, w_up_ref[:, cols], preferred_element_type=jnp.float32)
        a = (gate * jax.nn.sigmoid(gate) * up).astype(jnp.bfloat16)
        acc[...] += jnp.dot(a, w_down_ref[cols, :], preferred_element_type=jnp.float32)

    o_ref[...] = _rmsnorm(acc[...], gf_ref[...])


def kernel(x, norm1_g, w_in, pool_w, pool_scale, conv_w, w_out, norm2_g, w_gate, w_up, w_down, normf_g):
    batch, seq, d = x.shape
    assert d == D_MODEL and seq % TM == 0
    n_tok = batch * seq
    bf16 = jnp.bfloat16

    def resident(shape):
        return pl.BlockSpec(shape, lambda i: (0,) * len(shape), pipeline_mode=pl.Buffered(1))

    tile = pl.BlockSpec((TM, D_MODEL), lambda i: (i, 0))
    out = pl.pallas_call(
        functools.partial(_block_kernel, seq // TM),
        grid=(n_tok // TM,),
        in_specs=[
            tile,
            resident((1, D_MODEL)),
            resident((D_MODEL, IN_COLS)),
            resident((len(POOL_WINDOWS), POOL_GROUP_DIM, POOL_GROUP_DIM)),
            resident((1, POOL_WIDTH)),
            resident((CONV_K, CONV_WIDTH)),
            resident((POOL_WIDTH + CONV_WIDTH, D_MODEL)),
            resident((1, D_MODEL)),
            resident((D_MODEL, D_FF)),
            resident((D_MODEL, D_FF)),
            resident((D_FF, D_MODEL)),
            resident((1, D_MODEL)),
        ],
        out_specs=tile,
        out_shape=jax.ShapeDtypeStruct((n_tok, D_MODEL), x.dtype),
        scratch_shapes=[
            pltpu.VMEM((POOL_HALO + TM, POOL_WIDTH), jnp.float32),
            pltpu.VMEM((CONV_HALO + TM, CONV_WIDTH), jnp.float32),
            pltpu.VMEM((TM, POOL_WIDTH + CONV_WIDTH), bf16),
            pltpu.VMEM((TM, D_MODEL), jnp.float32),
        ],
        compiler_params=pltpu.CompilerParams(
            dimension_semantics=("arbitrary",),
            vmem_limit_bytes=VMEM_LIMIT_BYTES,
        ),
        name="hybrid_block",
    )(
        x.reshape(n_tok, D_MODEL),
        norm1_g.reshape(1, D_MODEL),
        w_in.astype(bf16),
        pool_w.astype(bf16),
        pool_scale.reshape(1, POOL_WIDTH),
        conv_w,
        w_out.astype(bf16),
        norm2_g.reshape(1, D_MODEL),
        w_gate.astype(bf16),
        w_up.astype(bf16),
        w_down.astype(bf16),
        normf_g.reshape(1, D_MODEL),
    )
    return out.reshape(batch, seq, D_MODEL)
```

```python
import functools

import jax
import jax.numpy as jnp
from jax import lax
from jax.experimental import pallas as pl
from jax.experimental.pallas import tpu as pltpu

D_MODEL = 1024
POOL_WIDTH = 512
CONV_WIDTH = 512
MIX_WIDTH = POOL_WIDTH + CONV_WIDTH
POOL_WINDOWS = (2, 4, 8, 16)
POOL_GROUP_DIM = 128
CONV_K = 3
IN_COLS = POOL_WIDTH + 3 * CONV_WIDTH
D_FF = 2816
RMS_EPS = 1e-6

TM = 512
POOL_HALO = 16
CONV_HALO = 8
FF_CHUNK = 256
STAGE_ROWS = 1024
STAGE_COLS = 512
STAGE_SLOTS = 4
VMEM_LIMIT_BYTES = 58 * 1024 * 1024


def _rmsnorm(x, g):
    ms = jnp.mean(x * x, axis=-1, keepdims=True)
    return (x * lax.rsqrt(ms + RMS_EPS)) * g


def _weight_chunks(w_in, w_out, w_gate, w_up, w_down, wb_in, wb_out, wb_gate, wb_up, wb_down):
    chunks = []
    for src, dst in ((w_in, wb_in), (w_out, wb_out), (w_gate, wb_gate), (w_up, wb_up), (w_down, wb_down)):
        n_rows, n_cols = src.shape
        for r0 in range(0, n_rows, STAGE_ROWS):
            rows = min(STAGE_ROWS, n_rows - r0)
            for c0 in range(0, n_cols, STAGE_COLS):
                cols = min(STAGE_COLS, n_cols - c0)
                idx = (pl.ds(r0, rows), pl.ds(c0, cols))
                chunks.append((src.at[idx], dst.at[idx], rows, cols))
    return chunks


def _load_weights(w_in, pool_w, w_out, w_gate, w_up, w_down,
                  wb_in, wb_pool, wb_out, wb_gate, wb_up, wb_down,
                  stage, pool_stage, sem, pool_sem):
    pool_copy = pltpu.make_async_copy(pool_w, pool_stage, pool_sem)
    pool_copy.start()
    chunks = _weight_chunks(w_in, w_out, w_gate, w_up, w_down, wb_in, wb_out, wb_gate, wb_up, wb_down)

    def copy(k):
        src, _, rows, cols = chunks[k]
        slot = k % STAGE_SLOTS
        return pltpu.make_async_copy(src, stage.at[slot, pl.ds(0, rows), pl.ds(0, cols)], sem.at[slot])

    for k in range(min(STAGE_SLOTS - 1, len(chunks))):
        copy(k).start()
    for k, (_, dst, rows, cols) in enumerate(chunks):
        if k + STAGE_SLOTS - 1 < len(chunks):
            copy(k + STAGE_SLOTS - 1).start()
        copy(k).wait()
        dst[...] = stage[k % STAGE_SLOTS, 0:rows, 0:cols].astype(jnp.bfloat16)
    pool_copy.wait()
    wb_pool[...] = pool_stage[...].astype(jnp.bfloat16)


def _block_kernel(tiles_per_seq,
                  x_ref, g1_ref, w_in_hbm, pool_w_hbm, pool_scale_ref, conv_w_ref,
                  w_out_hbm, g2_ref, w_gate_hbm, w_up_hbm, w_down_hbm, gf_ref,
                  o_ref,
                  w_in_ref, pool_w_ref, w_out_ref, w_gate_ref, w_up_ref, w_down_ref,
                  stage, pool_stage, sem, pool_sem,
                  vbuf, ubuf, ymix, acc):
    i = pl.program_id(0)

    @pl.when(i == 0)
    def _():
        _load_weights(w_in_hbm, pool_w_hbm, w_out_hbm, w_gate_hbm, w_up_hbm, w_down_hbm,
                      w_in_ref, pool_w_ref, w_out_ref, w_gate_ref, w_up_ref, w_down_ref,
                      stage, pool_stage, sem, pool_sem)

    @pl.when(i % tiles_per_seq == 0)
    def _():
        vbuf[0:POOL_HALO, :] = jnp.zeros((POOL_HALO, POOL_WIDTH), jnp.float32)
        ubuf[0:CONV_HALO, :] = jnp.zeros((CONV_HALO, CONV_WIDTH), jnp.float32)

    x = x_ref[...]
    hn = _rmsnorm(x, g1_ref[...]).astype(jnp.bfloat16)
    proj = jnp.dot(hn, w_in_ref[...], preferred_element_type=jnp.float32)

    vbuf[POOL_HALO:POOL_HALO + TM, :] = proj[:, :POOL_WIDTH]
    row = i % tiles_per_seq * TM + lax.broadcasted_iota(jnp.int32, (TM, 1), 0)
    for g, w in enumerate(POOL_WINDOWS):
        cols = slice(g * POOL_GROUP_DIM, (g + 1) * POOL_GROUP_DIM)
        cur = vbuf[POOL_HALO:POOL_HALO + TM, cols]
        s = cur
        for j in range(1, w):
            s = s + vbuf[POOL_HALO - j:POOL_HALO - j + TM, cols]
        cnt = jnp.minimum(row + 1, w).astype(jnp.float32)
        pooled = (s / cnt - cur).astype(jnp.bfloat16)
        mixed = jnp.dot(pooled, pool_w_ref[g], preferred_element_type=jnp.float32)
        ymix[:, cols] = (mixed * pool_scale_ref[:, cols]).astype(jnp.bfloat16)

    gb = proj[:, POOL_WIDTH:POOL_WIDTH + CONV_WIDTH]
    gc = proj[:, POOL_WIDTH + CONV_WIDTH:POOL_WIDTH + 2 * CONV_WIDTH]
    h = proj[:, POOL_WIDTH + 2 * CONV_WIDTH:]
    ubuf[CONV_HALO:CONV_HALO + TM, :] = gc * h
    conv = ubuf[CONV_HALO:CONV_HALO + TM, :] * conv_w_ref[CONV_K - 1:CONV_K, :]
    for k in range(CONV_K - 1):
        lag = CONV_K - 1 - k
        conv = conv + ubuf[CONV_HALO - lag:CONV_HALO - lag + TM, :] * conv_w_ref[k:k + 1, :]
    ymix[:, POOL_WIDTH:] = (gb * conv).astype(jnp.bfloat16)

    vbuf[0:POOL_HALO, :] = vbuf[TM:TM + POOL_HALO, :]
    ubuf[0:CONV_HALO, :] = ubuf[TM:TM + CONV_HALO, :]

    x1 = x + jnp.dot(ymix[...], w_out_ref[...], preferred_element_type=jnp.float32)

    hn2 = _rmsnorm(x1, g2_ref[...]).astype(jnp.bfloat16)
    acc[...] = x1
    for c in range(D_FF // FF_CHUNK):
        cols = slice(c * FF_CHUNK, (c + 1) * FF_CHUNK)
        gate = jnp.dot(hn2, w_gate_ref[:, cols], preferred_element_type=jnp.float32)
        up = jnp.dot(hn2, w_up_ref[:, cols], preferred_element_type=jnp.float32)
        a = (gate * jax.nn.sigmoid(gate) * up).astype(jnp.bfloat16)
        acc[...] += jnp.dot(a, w_down_ref[cols, :], preferred_element_type=jnp.float32)

    o_ref[...] = _rmsnorm(acc[...], gf_ref[...])


def kernel(x, norm1_g, w_in, pool_w, pool_scale, conv_w, w_out, norm2_g, w_gate, w_up, w_down, normf_g):
    batch, seq, d = x.shape
    assert d == D_MODEL and seq % TM == 0
    n_tok = batch * seq
    bf16 = jnp.bfloat16
    n_groups = len(POOL_WINDOWS)

    def resident(shape):
        return pl.BlockSpec(shape, lambda i: (0,) * len(shape), pipeline_mode=pl.Buffered(1))

    hbm = pl.BlockSpec(memory_space=pl.ANY)
    tile = pl.BlockSpec((TM, D_MODEL), lambda i: (i, 0))
    out = pl.pallas_call(
        functools.partial(_block_kernel, seq // TM),
        grid=(n_tok // TM,),
        in_specs=[
            tile,
            resident((1, D_MODEL)),
            hbm,
            hbm,
            resident((1, POOL_WIDTH)),
            resident((CONV_K, CONV_WIDTH)),
            hbm,
            resident((1, D_MODEL)),
            hbm,
            hbm,
            hbm,
            resident((1, D_MODEL)),
        ],
        out_specs=tile,
        out_shape=jax.ShapeDtypeStruct((n_tok, D_MODEL), x.dtype),
        scratch_shapes=[
            pltpu.VMEM((D_MODEL, IN_COLS), bf16),
            pltpu.VMEM((n_groups, POOL_GROUP_DIM, POOL_GROUP_DIM), bf16),
            pltpu.VMEM((MIX_WIDTH, D_MODEL), bf16),
            pltpu.VMEM((D_MODEL, D_FF), bf16),
            pltpu.VMEM((D_MODEL, D_FF), bf16),
            pltpu.VMEM((D_FF, D_MODEL), bf16),
            pltpu.VMEM((STAGE_SLOTS, STAGE_ROWS, STAGE_COLS), jnp.float32),
            pltpu.VMEM((n_groups, POOL_GROUP_DIM, POOL_GROUP_DIM), jnp.float32),
            pltpu.SemaphoreType.DMA((STAGE_SLOTS,)),
            pltpu.SemaphoreType.DMA(()),
            pltpu.VMEM((POOL_HALO + TM, POOL_WIDTH), jnp.float32),
            pltpu.VMEM((CONV_HALO + TM, CONV_WIDTH), jnp.float32),
            pltpu.VMEM((TM, MIX_WIDTH), bf16),
            pltpu.VMEM((TM, D_MODEL), jnp.float32),
        ],
        compiler_params=pltpu.CompilerParams(
            dimension_semantics=("arbitrary",),
            vmem_limit_bytes=VMEM_LIMIT_BYTES,
        ),
        name="hybrid_block",
    )(
        x.reshape(n_tok, D_MODEL),
        norm1_g.reshape(1, D_MODEL),
        w_in,
        pool_w,
        pool_scale.reshape(1, POOL_WIDTH),
        conv_w,
        w_out,
        norm2_g.reshape(1, D_MODEL),
        w_gate,
        w_up,
        w_down,
        normf_g.reshape(1, D_MODEL),
    )
    return out.reshape(batch, seq, D_MODEL)
```

```python
import functools

import jax
import jax.numpy as jnp
from jax import lax
from jax.experimental import pallas as pl
from jax.experimental.pallas import tpu as pltpu

D_MODEL = 1024
POOL_WIDTH = 512
CONV_WIDTH = 512
MIX_WIDTH = POOL_WIDTH + CONV_WIDTH
POOL_WINDOWS = (2, 4, 8, 16)
POOL_GROUP_DIM = 128
CONV_K = 3
IN_COLS = POOL_WIDTH + 3 * CONV_WIDTH
D_FF = 2816
RMS_EPS = 1e-6

SUB = 512
SUBTILES = 2
TM = SUB * SUBTILES
POOL_HALO = 16
CONV_HALO = 8
FF_CHUNK = 256
STAGE_ROWS = 1024
STAGE_COLS = 512
STAGE_SLOTS = 3
VMEM_LIMIT_BYTES = 60 * 1024 * 1024


def _rmsnorm(x, g):
    ms = jnp.mean(x * x, axis=-1, keepdims=True)
    return (x * lax.rsqrt(ms + RMS_EPS)) * g


def _weight_chunks(w_in, w_out, w_gate, w_up, w_down, wb_in, wb_out, wb_gate, wb_up, wb_down):
    chunks = []
    for src, dst in ((w_in, wb_in), (w_out, wb_out), (w_gate, wb_gate), (w_up, wb_up), (w_down, wb_down)):
        n_rows, n_cols = src.shape
        for r0 in range(0, n_rows, STAGE_ROWS):
            rows = min(STAGE_ROWS, n_rows - r0)
            for c0 in range(0, n_cols, STAGE_COLS):
                cols = min(STAGE_COLS, n_cols - c0)
                idx = (pl.ds(r0, rows), pl.ds(c0, cols))
                chunks.append((src.at[idx], dst.at[idx], rows, cols))
    return chunks


def _load_weights(w_in, pool_w, w_out, w_gate, w_up, w_down,
                  wb_in, wb_pool, wb_out, wb_gate, wb_up, wb_down,
                  stage, pool_stage, sem, pool_sem):
    pool_copy = pltpu.make_async_copy(pool_w, pool_stage, pool_sem)
    pool_copy.start()
    chunks = _weight_chunks(w_in, w_out, w_gate, w_up, w_down, wb_in, wb_out, wb_gate, wb_up, wb_down)

    def copy(k):
        src, _, rows, cols = chunks[k]
        slot = k % STAGE_SLOTS
        return pltpu.make_async_copy(src, stage.at[slot, pl.ds(0, rows), pl.ds(0, cols)], sem.at[slot])

    for k in range(min(STAGE_SLOTS - 1, len(chunks))):
        copy(k).start()
    for k, (_, dst, rows, cols) in enumerate(chunks):
        if k + STAGE_SLOTS - 1 < len(chunks):
            copy(k + STAGE_SLOTS - 1).start()
        copy(k).wait()
        dst[...] = stage[k % STAGE_SLOTS, 0:rows, 0:cols].astype(jnp.bfloat16)
    pool_copy.wait()
    wb_pool[...] = pool_stage[...].astype(jnp.bfloat16)


def _block_kernel(tiles_per_seq,
                  x_ref, g1_ref, w_in_hbm, pool_w_hbm, pool_scale_ref, conv_w_ref,
                  w_out_hbm, g2_ref, w_gate_hbm, w_up_hbm, w_down_hbm, gf_ref,
                  o_ref,
                  w_in_ref, pool_w_ref, w_out_ref, w_gate_ref, w_up_ref, w_down_ref,
                  stage, pool_stage, sem, pool_sem,
                  vbuf, ubuf, gbbuf, ymix):
    i = pl.program_id(0)
    f32 = jnp.float32
    bf16 = jnp.bfloat16

    @pl.when(i == 0)
    def _():
        _load_weights(w_in_hbm, pool_w_hbm, w_out_hbm, w_gate_hbm, w_up_hbm, w_down_hbm,
                      w_in_ref, pool_w_ref, w_out_ref, w_gate_ref, w_up_ref, w_down_ref,
                      stage, pool_stage, sem, pool_sem)

    @pl.when(i % tiles_per_seq == 0)
    def _():
        vbuf[0:POOL_HALO, :] = jnp.zeros((POOL_HALO, POOL_WIDTH), f32)
        ubuf[0:CONV_HALO, :] = jnp.zeros((CONV_HALO, CONV_WIDTH), f32)

    def in_proj(s):
        rows = pl.ds(s * SUB, SUB)
        hn = _rmsnorm(x_ref[rows, :], g1_ref[...]).astype(bf16)

        def section(k):
            cols = slice(k * POOL_WIDTH, (k + 1) * POOL_WIDTH)
            return jnp.dot(hn, w_in_ref[:, cols], preferred_element_type=f32)

        vbuf[pl.ds(POOL_HALO + s * SUB, SUB), :] = section(0)
        gbbuf[rows, :] = section(1)
        urows = pl.ds(CONV_HALO + s * SUB, SUB)
        ubuf[urows, :] = section(2)
        ubuf[urows, :] = ubuf[urows, :] * section(3)

    def mixers(s):
        rows = pl.ds(s * SUB, SUB)
        v0 = POOL_HALO + s * SUB
        u0 = CONV_HALO + s * SUB
        pos = (i % tiles_per_seq) * TM + s * SUB + lax.broadcasted_iota(jnp.int32, (SUB, 1), 0)
        for g, w in enumerate(POOL_WINDOWS):
            cols = slice(g * POOL_GROUP_DIM, (g + 1) * POOL_GROUP_DIM)
            cur = vbuf[pl.ds(v0, SUB), cols]
            tot = cur
            for j in range(1, w):
                tot = tot + vbuf[pl.ds(v0 - j, SUB), cols]
            cnt = jnp.minimum(pos + 1, w).astype(f32)
            pooled = (tot / cnt - cur).astype(bf16)
            mixed = jnp.dot(pooled, pool_w_ref[g], preferred_element_type=f32)
            ymix[rows, cols] = (mixed * pool_scale_ref[:, cols]).astype(bf16)
        conv = ubuf[pl.ds(u0, SUB), :] * conv_w_ref[CONV_K - 1:CONV_K, :]
        for k in range(CONV_K - 1):
            lag = CONV_K - 1 - k
            conv = conv + ubuf[pl.ds(u0 - lag, SUB), :] * conv_w_ref[k:k + 1, :]
        ymix[rows, POOL_WIDTH:] = (gbbuf[rows, :] * conv).astype(bf16)

    def out_proj(s):
        rows = pl.ds(s * SUB, SUB)
        o_ref[rows, :] = x_ref[rows, :] + jnp.dot(ymix[rows, :], w_out_ref[...],
                                                  preferred_element_type=f32)

    def ffn(s):
        rows = pl.ds(s * SUB, SUB)
        hn2 = _rmsnorm(o_ref[rows, :], g2_ref[...]).astype(bf16)
        for c in range(D_FF // FF_CHUNK):
            cols = slice(c * FF_CHUNK, (c + 1) * FF_CHUNK)
            gate = jnp.dot(hn2, w_gate_ref[:, cols], preferred_element_type=f32)
            up = jnp.dot(hn2, w_up_ref[:, cols], preferred_element_type=f32)
            a = (gate * jax.nn.sigmoid(gate) * up).astype(bf16)
            o_ref[rows, :] += jnp.dot(a, w_down_ref[cols, :], preferred_element_type=f32)

    def final_norm(s):
        rows = pl.ds(s * SUB, SUB)
        o_ref[rows, :] = _rmsnorm(o_ref[rows, :], gf_ref[...])

    for s in range(SUBTILES):
        in_proj(s)
    for s in range(SUBTILES):
        mixers(s)
        out_proj(s)
    vbuf[0:POOL_HALO, :] = vbuf[TM:TM + POOL_HALO, :]
    ubuf[0:CONV_HALO, :] = ubuf[TM:TM + CONV_HALO, :]
    for s in range(SUBTILES):
        ffn(s)
        final_norm(s)


def kernel(x, norm1_g, w_in, pool_w, pool_scale, conv_w, w_out, norm2_g, w_gate, w_up, w_down, normf_g):
    batch, seq, d = x.shape
    assert d == D_MODEL and seq % TM == 0
    n_tok = batch * seq
    bf16 = jnp.bfloat16
    n_groups = len(POOL_WINDOWS)

    def resident(shape):
        return pl.BlockSpec(shape, lambda i: (0,) * len(shape), pipeline_mode=pl.Buffered(1))

    hbm = pl.BlockSpec(memory_space=pl.ANY)
    tile = pl.BlockSpec((TM, D_MODEL), lambda i: (i, 0))
    out = pl.pallas_call(
        functools.partial(_block_kernel, seq // TM),
        grid=(n_tok // TM,),
        in_specs=[
            tile,
            resident((1, D_MODEL)),
            hbm,
            hbm,
            resident((1, POOL_WIDTH)),
            resident((CONV_K, CONV_WIDTH)),
            hbm,
            resident((1, D_MODEL)),
            hbm,
            hbm,
            hbm,
            resident((1, D_MODEL)),
        ],
        out_specs=tile,
        out_shape=jax.ShapeDtypeStruct((n_tok, D_MODEL), x.dtype),
        scratch_shapes=[
            pltpu.VMEM((D_MODEL, IN_COLS), bf16),
            pltpu.VMEM((n_groups, POOL_GROUP_DIM, POOL_GROUP_DIM), bf16),
            pltpu.VMEM((MIX_WIDTH, D_MODEL), bf16),
            pltpu.VMEM((D_MODEL, D_FF), bf16),
            pltpu.VMEM((D_MODEL, D_FF), bf16),
            pltpu.VMEM((D_FF, D_MODEL), bf16),
            pltpu.VMEM((STAGE_SLOTS, STAGE_ROWS, STAGE_COLS), jnp.float32),
            pltpu.VMEM((n_groups, POOL_GROUP_DIM, POOL_GROUP_DIM), jnp.float32),
            pltpu.SemaphoreType.DMA((STAGE_SLOTS,)),
            pltpu.SemaphoreType.DMA(()),
            pltpu.VMEM((POOL_HALO + TM, POOL_WIDTH), jnp.float32),
            pltpu.VMEM((CONV_HALO + TM, CONV_WIDTH), jnp.float32),
            pltpu.VMEM((TM, CONV_WIDTH), jnp.float32),
            pltpu.VMEM((TM, MIX_WIDTH), bf16),
        ],
        compiler_params=pltpu.CompilerParams(
            dimension_semantics=("arbitrary",),
            vmem_limit_bytes=VMEM_LIMIT_BYTES,
        ),
        name="hybrid_block",
    )(
        x.reshape(n_tok, D_MODEL),
        norm1_g.reshape(1, D_MODEL),
        w_in,
        pool_w,
        pool_scale.reshape(1, POOL_WIDTH),
        conv_w,
        w_out,
        norm2_g.reshape(1, D_MODEL),
        w_gate,
        w_up,
        w_down,
        normf_g.reshape(1, D_MODEL),
    )
    return out.reshape(batch, seq, D_MODEL)
```

```python
import functools

import jax
import jax.numpy as jnp
from jax import lax
from jax.experimental import pallas as pl
from jax.experimental.pallas import tpu as pltpu

D_MODEL = 1024
POOL_WIDTH = 512
CONV_WIDTH = 512
MIX_WIDTH = POOL_WIDTH + CONV_WIDTH
POOL_WINDOWS = (2, 4, 8, 16)
POOL_GROUP_DIM = 128
CONV_K = 3
IN_COLS = POOL_WIDTH + 3 * CONV_WIDTH
D_FF = 2816
RMS_EPS = 1e-6

SUB = 1024
SUBTILES = 1
TM = SUB * SUBTILES
POOL_HALO = 16
CONV_HALO = 8
FF_CHUNK = 256
STAGE_ROWS = 1024
STAGE_COLS = 512
STAGE_SLOTS = 3
VMEM_LIMIT_BYTES = 60 * 1024 * 1024


def _rmsnorm(x, g):
    ms = jnp.mean(x * x, axis=-1, keepdims=True)
    return (x * lax.rsqrt(ms + RMS_EPS)) * g


def _weight_chunks(w_in, w_out, w_gate, w_up, w_down, wb_in, wb_out, wb_gate, wb_up, wb_down):
    chunks = []
    for src, dst in ((w_in, wb_in), (w_out, wb_out), (w_gate, wb_gate), (w_up, wb_up), (w_down, wb_down)):
        n_rows, n_cols = src.shape
        for r0 in range(0, n_rows, STAGE_ROWS):
            rows = min(STAGE_ROWS, n_rows - r0)
            for c0 in range(0, n_cols, STAGE_COLS):
                cols = min(STAGE_COLS, n_cols - c0)
                idx = (pl.ds(r0, rows), pl.ds(c0, cols))
                chunks.append((src.at[idx], dst.at[idx], rows, cols))
    return chunks


def _load_weights(w_in, pool_w, w_out, w_gate, w_up, w_down,
                  wb_in, wb_pool, wb_out, wb_gate, wb_up, wb_down,
                  stage, pool_stage, sem, pool_sem):
    pool_copy = pltpu.make_async_copy(pool_w, pool_stage, pool_sem)
    pool_copy.start()
    chunks = _weight_chunks(w_in, w_out, w_gate, w_up, w_down, wb_in, wb_out, wb_gate, wb_up, wb_down)

    def copy(k):
        src, _, rows, cols = chunks[k]
        slot = k % STAGE_SLOTS
        return pltpu.make_async_copy(src, stage.at[slot, pl.ds(0, rows), pl.ds(0, cols)], sem.at[slot])

    for k in range(min(STAGE_SLOTS - 1, len(chunks))):
        copy(k).start()
    for k, (_, dst, rows, cols) in enumerate(chunks):
        if k + STAGE_SLOTS - 1 < len(chunks):
            copy(k + STAGE_SLOTS - 1).start()
        copy(k).wait()
        dst[...] = stage[k % STAGE_SLOTS, 0:rows, 0:cols].astype(jnp.bfloat16)
    pool_copy.wait()
    wb_pool[...] = pool_stage[...].astype(jnp.bfloat16)


def _block_kernel(tiles_per_seq,
                  x_ref, g1_ref, w_in_hbm, pool_w_hbm, pool_scale_ref, conv_w_ref,
                  w_out_hbm, g2_ref, w_gate_hbm, w_up_hbm, w_down_hbm, gf_ref,
                  o_ref,
                  w_in_ref, pool_w_ref, w_out_ref, w_gate_ref, w_up_ref, w_down_ref,
                  stage, pool_stage, sem, pool_sem,
                  vbuf, ubuf, gbbuf, ymix):
    i = pl.program_id(0)
    f32 = jnp.float32
    bf16 = jnp.bfloat16

    @pl.when(i == 0)
    def _():
        _load_weights(w_in_hbm, pool_w_hbm, w_out_hbm, w_gate_hbm, w_up_hbm, w_down_hbm,
                      w_in_ref, pool_w_ref, w_out_ref, w_gate_ref, w_up_ref, w_down_ref,
                      stage, pool_stage, sem, pool_sem)

    @pl.when(i % tiles_per_seq == 0)
    def _():
        vbuf[0:POOL_HALO, :] = jnp.zeros((POOL_HALO, POOL_WIDTH), f32)
        ubuf[0:CONV_HALO, :] = jnp.zeros((CONV_HALO, CONV_WIDTH), f32)

    def in_proj(s):
        rows = pl.ds(s * SUB, SUB)
        hn = _rmsnorm(x_ref[rows, :], g1_ref[...]).astype(bf16)

        def section(k):
            cols = slice(k * POOL_WIDTH, (k + 1) * POOL_WIDTH)
            return jnp.dot(hn, w_in_ref[:, cols], preferred_element_type=f32)

        vbuf[pl.ds(POOL_HALO + s * SUB, SUB), :] = section(0)
        gbbuf[rows, :] = section(1)
        urows = pl.ds(CONV_HALO + s * SUB, SUB)
        ubuf[urows, :] = section(2)
        ubuf[urows, :] = ubuf[urows, :] * section(3)

    def mixers(s):
        rows = pl.ds(s * SUB, SUB)
        v0 = POOL_HALO + s * SUB
        u0 = CONV_HALO + s * SUB
        pos = (i % tiles_per_seq) * TM + s * SUB + lax.broadcasted_iota(jnp.int32, (SUB, 1), 0)
        for g, w in enumerate(POOL_WINDOWS):
            cols = slice(g * POOL_GROUP_DIM, (g + 1) * POOL_GROUP_DIM)
            cur = vbuf[pl.ds(v0, SUB), cols]
            tot = cur
            for j in range(1, w):
                tot = tot + vbuf[pl.ds(v0 - j, SUB), cols]
            cnt = jnp.minimum(pos + 1, w).astype(f32)
            pooled = (tot / cnt - cur).astype(bf16)
            mixed = jnp.dot(pooled, pool_w_ref[g], preferred_element_type=f32)
            ymix[rows, cols] = (mixed * pool_scale_ref[:, cols]).astype(bf16)
        conv = ubuf[pl.ds(u0, SUB), :] * conv_w_ref[CONV_K - 1:CONV_K, :]
        for k in range(CONV_K - 1):
            lag = CONV_K - 1 - k
            conv = conv + ubuf[pl.ds(u0 - lag, SUB), :] * conv_w_ref[k:k + 1, :]
        ymix[rows, POOL_WIDTH:] = (gbbuf[rows, :] * conv).astype(bf16)

    def out_proj(s):
        rows = pl.ds(s * SUB, SUB)
        o_ref[rows, :] = x_ref[rows, :] + jnp.dot(ymix[rows, :], w_out_ref[...],
                                                  preferred_element_type=f32)

    def ffn(s):
        rows = pl.ds(s * SUB, SUB)
        hn2 = _rmsnorm(o_ref[rows, :], g2_ref[...]).astype(bf16)
        for c in range(D_FF // FF_CHUNK):
            cols = slice(c * FF_CHUNK, (c + 1) * FF_CHUNK)
            gate = jnp.dot(hn2, w_gate_ref[:, cols], preferred_element_type=f32)
            up = jnp.dot(hn2, w_up_ref[:, cols], preferred_element_type=f32)
            a = (gate * jax.nn.sigmoid(gate) * up).astype(bf16)
            o_ref[rows, :] += jnp.dot(a, w_down_ref[cols, :], preferred_element_type=f32)

    def final_norm(s):
        rows = pl.ds(s * SUB, SUB)
        o_ref[rows, :] = _rmsnorm(o_ref[rows, :], gf_ref[...])

    for s in range(SUBTILES):
        in_proj(s)
    for s in range(SUBTILES):
        mixers(s)
        out_proj(s)
    vbuf[0:POOL_HALO, :] = vbuf[TM:TM + POOL_HALO, :]
    ubuf[0:CONV_HALO, :] = ubuf[TM:TM + CONV_HALO, :]
    for s in range(SUBTILES):
        ffn(s)
        final_norm(s)


def kernel(x, norm1_g, w_in, pool_w, pool_scale, conv_w, w_out, norm2_g, w_gate, w_up, w_down, normf_g):
    batch, seq, d = x.shape
    assert d == D_MODEL and seq % TM == 0
    n_tok = batch * seq
    bf16 = jnp.bfloat16
    n_groups = len(POOL_WINDOWS)

    def resident(shape):
        return pl.BlockSpec(shape, lambda i: (0,) * len(shape), pipeline_mode=pl.Buffered(1))

    hbm = pl.BlockSpec(memory_space=pl.ANY)
    tile = pl.BlockSpec((TM, D_MODEL), lambda i: (i, 0))
    out = pl.pallas_call(
        functools.partial(_block_kernel, seq // TM),
        grid=(n_tok // TM,),
        in_specs=[
            tile,
            resident((1, D_MODEL)),
            hbm,
            hbm,
            resident((1, POOL_WIDTH)),
            resident((CONV_K, CONV_WIDTH)),
            hbm,
            resident((1, D_MODEL)),
            hbm,
            hbm,
            hbm,
            resident((1, D_MODEL)),
        ],
        out_specs=tile,
        out_shape=jax.ShapeDtypeStruct((n_tok, D_MODEL), x.dtype),
        scratch_shapes=[
            pltpu.VMEM((D_MODEL, IN_COLS), bf16),
            pltpu.VMEM((n_groups, POOL_GROUP_DIM, POOL_GROUP_DIM), bf16),
            pltpu.VMEM((MIX_WIDTH, D_MODEL), bf16),
            pltpu.VMEM((D_MODEL, D_FF), bf16),
            pltpu.VMEM((D_MODEL, D_FF), bf16),
            pltpu.VMEM((D_FF, D_MODEL), bf16),
            pltpu.VMEM((STAGE_SLOTS, STAGE_ROWS, STAGE_COLS), jnp.float32),
            pltpu.VMEM((n_groups, POOL_GROUP_DIM, POOL_GROUP_DIM), jnp.float32),
            pltpu.SemaphoreType.DMA((STAGE_SLOTS,)),
            pltpu.SemaphoreType.DMA(()),
            pltpu.VMEM((POOL_HALO + TM, POOL_WIDTH), jnp.float32),
            pltpu.VMEM((CONV_HALO + TM, CONV_WIDTH), jnp.float32),
            pltpu.VMEM((TM, CONV_WIDTH), jnp.float32),
            pltpu.VMEM((TM, MIX_WIDTH), bf16),
        ],
        compiler_params=pltpu.CompilerParams(
            dimension_semantics=("arbitrary",),
            vmem_limit_bytes=VMEM_LIMIT_BYTES,
        ),
        name="hybrid_block",
    )(
        x.reshape(n_tok, D_MODEL),
        norm1_g.reshape(1, D_MODEL),
        w_in,
        pool_w,
        pool_scale.reshape(1, POOL_WIDTH),
        conv_w,
        w_out,
        norm2_g.reshape(1, D_MODEL),
        w_gate,
        w_up,
        w_down,
        normf_g.reshape(1, D_MODEL),
    )
    return out.reshape(batch, seq, D_MODEL)
```

```python
import functools

import jax
import jax.numpy as jnp
from jax import lax
from jax.experimental import pallas as pl
from jax.experimental.pallas import tpu as pltpu

D_MODEL = 1024
POOL_WIDTH = 512
CONV_WIDTH = 512
MIX_WIDTH = POOL_WIDTH + CONV_WIDTH
POOL_WINDOWS = (2, 4, 8, 16)
POOL_GROUP_DIM = 128
CONV_K = 3
IN_COLS = POOL_WIDTH + 3 * CONV_WIDTH
D_FF = 2816
RMS_EPS = 1e-6

SUB = 512
SUBTILES = 2
TM = SUB * SUBTILES
POOL_HALO = 16
CONV_HALO = 8
FF_CHUNK = 256
STAGE_ROWS = 1024
STAGE_COLS = 512
STAGE_SLOTS = 3
VMEM_LIMIT_BYTES = 60 * 1024 * 1024


def _rmsnorm(x, g):
    ms = jnp.mean(x * x, axis=-1, keepdims=True)
    return (x * lax.rsqrt(ms + RMS_EPS)) * g


def _weight_chunks(w_in, w_out, w_gate, w_up, w_down, wb_in, wb_out, wb_gate, wb_up, wb_down):
    chunks = []
    for src, dst in ((w_in, wb_in), (w_out, wb_out), (w_gate, wb_gate), (w_up, wb_up), (w_down, wb_down)):
        n_rows, n_cols = src.shape
        for r0 in range(0, n_rows, STAGE_ROWS):
            rows = min(STAGE_ROWS, n_rows - r0)
            for c0 in range(0, n_cols, STAGE_COLS):
                cols = min(STAGE_COLS, n_cols - c0)
                idx = (pl.ds(r0, rows), pl.ds(c0, cols))
                chunks.append((src.at[idx], dst.at[idx], rows, cols))
    return chunks


def _load_weights(w_in, pool_w, w_out, w_gate, w_up, w_down,
                  wb_in, wb_pool, wb_out, wb_gate, wb_up, wb_down,
                  stage, pool_stage, sem, pool_sem):
    pool_copy = pltpu.make_async_copy(pool_w, pool_stage, pool_sem)
    pool_copy.start()
    chunks = _weight_chunks(w_in, w_out, w_gate, w_up, w_down, wb_in, wb_out, wb_gate, wb_up, wb_down)

    def copy(k):
        src, _, rows, cols = chunks[k]
        slot = k % STAGE_SLOTS
        return pltpu.make_async_copy(src, stage.at[slot, pl.ds(0, rows), pl.ds(0, cols)], sem.at[slot])

    for k in range(min(STAGE_SLOTS - 1, len(chunks))):
        copy(k).start()
    for k, (_, dst, rows, cols) in enumerate(chunks):
        if k + STAGE_SLOTS - 1 < len(chunks):
            copy(k + STAGE_SLOTS - 1).start()
        copy(k).wait()
        dst[...] = stage[k % STAGE_SLOTS, 0:rows, 0:cols].astype(jnp.bfloat16)
    pool_copy.wait()
    wb_pool[...] = pool_stage[...].astype(jnp.bfloat16)


def _block_kernel(tiles_per_seq,
                  x_ref, g1_ref, w_in_hbm, pool_w_hbm, pool_scale_ref, conv_w_ref,
                  w_out_hbm, g2_ref, w_gate_hbm, w_up_hbm, w_down_hbm, gf_ref,
                  o_ref,
                  w_in_ref, pool_w_ref, w_out_ref, w_gate_ref, w_up_ref, w_down_ref,
                  stage, pool_stage, sem, pool_sem,
                  vbuf, ubuf, gbbuf, ymix, hn2buf):
    i = pl.program_id(0)
    f32 = jnp.float32
    bf16 = jnp.bfloat16

    @pl.when(i == 0)
    def _():
        _load_weights(w_in_hbm, pool_w_hbm, w_out_hbm, w_gate_hbm, w_up_hbm, w_down_hbm,
                      w_in_ref, pool_w_ref, w_out_ref, w_gate_ref, w_up_ref, w_down_ref,
                      stage, pool_stage, sem, pool_sem)

    @pl.when(i % tiles_per_seq == 0)
    def _():
        vbuf[0:POOL_HALO, :] = jnp.zeros((POOL_HALO, POOL_WIDTH), f32)
        ubuf[0:CONV_HALO, :] = jnp.zeros((CONV_HALO, CONV_WIDTH), f32)

    def in_proj(s):
        rows = pl.ds(s * SUB, SUB)
        hn = _rmsnorm(x_ref[rows, :], g1_ref[...]).astype(bf16)

        def section(k):
            cols = slice(k * POOL_WIDTH, (k + 1) * POOL_WIDTH)
            return jnp.dot(hn, w_in_ref[:, cols], preferred_element_type=f32)

        vbuf[pl.ds(POOL_HALO + s * SUB, SUB), :] = section(0)
        gbbuf[rows, :] = section(1)
        urows = pl.ds(CONV_HALO + s * SUB, SUB)
        ubuf[urows, :] = section(2)
        ubuf[urows, :] = ubuf[urows, :] * section(3)

    def pool_mixer(s):
        rows = pl.ds(s * SUB, SUB)
        v0 = POOL_HALO + s * SUB
        pos = (i % tiles_per_seq) * TM + s * SUB + lax.broadcasted_iota(jnp.int32, (SUB, 1), 0)
        for g, w in enumerate(POOL_WINDOWS):
            cols = slice(g * POOL_GROUP_DIM, (g + 1) * POOL_GROUP_DIM)
            cur = vbuf[pl.ds(v0, SUB), cols]
            tot = cur
            for j in range(1, w):
                tot = tot + vbuf[pl.ds(v0 - j, SUB), cols]
            cnt = jnp.minimum(pos + 1, w).astype(f32)
            pooled = (tot / cnt - cur).astype(bf16)
            mixed = jnp.dot(pooled, pool_w_ref[g], preferred_element_type=f32)
            ymix[rows, cols] = (mixed * pool_scale_ref[:, cols]).astype(bf16)

    def conv_mixer(s):
        rows = pl.ds(s * SUB, SUB)
        u0 = CONV_HALO + s * SUB
        conv = ubuf[pl.ds(u0, SUB), :] * conv_w_ref[CONV_K - 1:CONV_K, :]
        for k in range(CONV_K - 1):
            lag = CONV_K - 1 - k
            conv = conv + ubuf[pl.ds(u0 - lag, SUB), :] * conv_w_ref[k:k + 1, :]
        ymix[rows, POOL_WIDTH:] = (gbbuf[rows, :] * conv).astype(bf16)

    def out_proj(s):
        rows = pl.ds(s * SUB, SUB)
        o_ref[rows, :] = x_ref[rows, :] + jnp.dot(ymix[rows, :], w_out_ref[...],
                                                  preferred_element_type=f32)

    def ffn():
        for s in range(SUBTILES):
            rows = pl.ds(s * SUB, SUB)
            hn2buf[rows, :] = _rmsnorm(o_ref[rows, :], g2_ref[...]).astype(bf16)
        n_chunks = D_FF // FF_CHUNK
        for c in range(n_chunks):
            cols = slice(c * FF_CHUNK, (c + 1) * FF_CHUNK)
            gate = jnp.dot(hn2buf[...], w_gate_ref[:, cols], preferred_element_type=f32)
            up = jnp.dot(hn2buf[...], w_up_ref[:, cols], preferred_element_type=f32)
            a = (gate * jax.nn.sigmoid(gate) * up).astype(bf16)
            if c < n_chunks - 1:
                o_ref[...] += jnp.dot(a, w_down_ref[cols, :], preferred_element_type=f32)
            else:
                for s in range(SUBTILES):
                    rows = pl.ds(s * SUB, SUB)
                    y = o_ref[rows, :] + jnp.dot(a[s * SUB:(s + 1) * SUB], w_down_ref[cols, :],
                                                 preferred_element_type=f32)
                    o_ref[rows, :] = _rmsnorm(y, gf_ref[...])

    for s in range(SUBTILES):
        in_proj(s)
    for s in range(SUBTILES):
        pool_mixer(s)
    for s in range(SUBTILES):
        conv_mixer(s)
        out_proj(s)
    vbuf[0:POOL_HALO, :] = vbuf[TM:TM + POOL_HALO, :]
    ubuf[0:CONV_HALO, :] = ubuf[TM:TM + CONV_HALO, :]
    ffn()


def kernel(x, norm1_g, w_in, pool_w, pool_scale, conv_w, w_out, norm2_g, w_gate, w_up, w_down, normf_g):
    batch, seq, d = x.shape
    assert d == D_MODEL and seq % TM == 0
    n_tok = batch * seq
    bf16 = jnp.bfloat16
    n_groups = len(POOL_WINDOWS)

    def resident(shape):
        return pl.BlockSpec(shape, lambda i: (0,) * len(shape), pipeline_mode=pl.Buffered(1))

    hbm = pl.BlockSpec(memory_space=pl.ANY)
    tile = pl.BlockSpec((TM, D_MODEL), lambda i: (i, 0))
    out = pl.pallas_call(
        functools.partial(_block_kernel, seq // TM),
        grid=(n_tok // TM,),
        in_specs=[
            tile,
            resident((1, D_MODEL)),
            hbm,
            hbm,
            resident((1, POOL_WIDTH)),
            resident((CONV_K, CONV_WIDTH)),
            hbm,
            resident((1, D_MODEL)),
            hbm,
            hbm,
            hbm,
            resident((1, D_MODEL)),
        ],
        out_specs=tile,
        out_shape=jax.ShapeDtypeStruct((n_tok, D_MODEL), x.dtype),
        scratch_shapes=[
            pltpu.VMEM((D_MODEL, IN_COLS), bf16),
            pltpu.VMEM((n_groups, POOL_GROUP_DIM, POOL_GROUP_DIM), bf16),
            pltpu.VMEM((MIX_WIDTH, D_MODEL), bf16),
            pltpu.VMEM((D_MODEL, D_FF), bf16),
            pltpu.VMEM((D_MODEL, D_FF), bf16),
            pltpu.VMEM((D_FF, D_MODEL), bf16),
            pltpu.VMEM((STAGE_SLOTS, STAGE_ROWS, STAGE_COLS), jnp.float32),
            pltpu.VMEM((n_groups, POOL_GROUP_DIM, POOL_GROUP_DIM), jnp.float32),
            pltpu.SemaphoreType.DMA((STAGE_SLOTS,)),
            pltpu.SemaphoreType.DMA(()),
            pltpu.VMEM((POOL_HALO + TM, POOL_WIDTH), jnp.float32),
            pltpu.VMEM((CONV_HALO + TM, CONV_WIDTH), jnp.float32),
            pltpu.VMEM((TM, CONV_WIDTH), jnp.float32),
            pltpu.VMEM((TM, MIX_WIDTH), bf16),
            pltpu.VMEM((TM, D_MODEL), bf16),
        ],
        compiler_params=pltpu.CompilerParams(
            dimension_semantics=("arbitrary",),
            vmem_limit_bytes=VMEM_LIMIT_BYTES,
        ),
        name="hybrid_block",
    )(
        x.reshape(n_tok, D_MODEL),
        norm1_g.reshape(1, D_MODEL),
        w_in,
        pool_w,
        pool_scale.reshape(1, POOL_WIDTH),
        conv_w,
        w_out,
        norm2_g.reshape(1, D_MODEL),
        w_gate,
        w_up,
        w_down,
        normf_g.reshape(1, D_MODEL),
    )
    return out.reshape(batch, seq, D_MODEL)
```

```python
import functools

import jax
import jax.numpy as jnp
from jax import lax
from jax.experimental import pallas as pl
from jax.experimental.pallas import tpu as pltpu

D_MODEL = 1024
POOL_WIDTH = 512
CONV_WIDTH = 512
MIX_WIDTH = POOL_WIDTH + CONV_WIDTH
POOL_WINDOWS = (2, 4, 8, 16)
POOL_GROUP_DIM = 128
CONV_K = 3
IN_COLS = POOL_WIDTH + 3 * CONV_WIDTH
D_FF = 2816
RMS_EPS = 1e-6

SUB = 512
SUBTILES = 2
TM = SUB * SUBTILES
POOL_HALO = 16
CONV_HALO = 8
FF_CHUNK = 256
FINAL_ROWS = 256
STAGE_ROWS = 1024
STAGE_COLS = 512
STAGE_SLOTS = 3
VMEM_LIMIT_BYTES = 60 * 1024 * 1024


def _rmsnorm(x, g):
    ms = jnp.mean(x * x, axis=-1, keepdims=True)
    return (x * lax.rsqrt(ms + RMS_EPS)) * g


def _weight_chunks(w_in, w_out, w_gate, w_up, w_down, wb_in, wb_out, wb_gate, wb_up, wb_down):
    assert STAGE_COLS == POOL_WIDTH and STAGE_ROWS >= D_MODEL
    chunks = []
    for src, dst in ((w_in, wb_in), (w_out, wb_out), (w_gate, wb_gate), (w_up, wb_up), (w_down, wb_down)):
        n_rows, n_cols = src.shape
        for r0 in range(0, n_rows, STAGE_ROWS):
            rows = min(STAGE_ROWS, n_rows - r0)
            for c0 in range(0, n_cols, STAGE_COLS):
                cols = min(STAGE_COLS, n_cols - c0)
                idx = (pl.ds(r0, rows), pl.ds(c0, cols))
                chunks.append((src.at[idx], dst.at[idx], rows, cols, src is w_in and c0 == 0))
    return chunks


def _load_weights(w_in, pool_w, pool_scale_ref, w_out, w_gate, w_up, w_down,
                  wb_in, wb_out, wb_gate, wb_up, wb_down,
                  stage, pool_stage, sem, pool_sem):
    pool_copy = pltpu.make_async_copy(pool_w, pool_stage, pool_sem)
    pool_copy.start()
    chunks = _weight_chunks(w_in, w_out, w_gate, w_up, w_down, wb_in, wb_out, wb_gate, wb_up, wb_down)

    def copy(k):
        src, _, rows, cols, _ = chunks[k]
        slot = k % STAGE_SLOTS
        return pltpu.make_async_copy(src, stage.at[slot, pl.ds(0, rows), pl.ds(0, cols)], sem.at[slot])

    for k in range(min(STAGE_SLOTS - 1, len(chunks))):
        copy(k).start()
    for k, (_, dst, rows, cols, is_pool_section) in enumerate(chunks):
        if k + STAGE_SLOTS - 1 < len(chunks):
            copy(k + STAGE_SLOTS - 1).start()
        copy(k).wait()
        staged = stage.at[k % STAGE_SLOTS]
        if is_pool_section:
            pool_copy.wait()
            for g in range(len(POOL_WINDOWS)):
                gcols = slice(g * POOL_GROUP_DIM, (g + 1) * POOL_GROUP_DIM)
                group_map = pool_stage[g] * pool_scale_ref[:, gcols]
                dst[:, gcols] = jnp.dot(staged[0:rows, gcols], group_map,
                                        preferred_element_type=jnp.float32,
                                        precision=lax.Precision.HIGHEST).astype(jnp.bfloat16)
        else:
            dst[...] = staged[0:rows, 0:cols].astype(jnp.bfloat16)


def _block_kernel(tiles_per_seq,
                  x_ref, g1_ref, w_in_hbm, pool_w_hbm, pool_scale_ref, conv_w_ref,
                  w_out_hbm, g2_ref, w_gate_hbm, w_up_hbm, w_down_hbm, gf_ref,
                  o_ref,
                  w_in_ref, w_out_ref, w_gate_ref, w_up_ref, w_down_ref,
                  stage, pool_stage, sem, pool_sem,
                  vbuf, ubuf, gbbuf, ymix, hn2buf):
    i = pl.program_id(0)
    f32 = jnp.float32
    bf16 = jnp.bfloat16

    @pl.when(i == 0)
    def _():
        _load_weights(w_in_hbm, pool_w_hbm, pool_scale_ref, w_out_hbm, w_gate_hbm, w_up_hbm, w_down_hbm,
                      w_in_ref, w_out_ref, w_gate_ref, w_up_ref, w_down_ref,
                      stage, pool_stage, sem, pool_sem)

    @pl.when(i % tiles_per_seq == 0)
    def _():
        vbuf[0:POOL_HALO, :] = jnp.zeros((POOL_HALO, POOL_WIDTH), f32)
        ubuf[0:CONV_HALO, :] = jnp.zeros((CONV_HALO, CONV_WIDTH), f32)

    def in_proj(s):
        rows = pl.ds(s * SUB, SUB)
        hn = _rmsnorm(x_ref[rows, :], g1_ref[...]).astype(bf16)

        def section(k):
            cols = slice(k * POOL_WIDTH, (k + 1) * POOL_WIDTH)
            return jnp.dot(hn, w_in_ref[:, cols], preferred_element_type=f32)

        vbuf[pl.ds(POOL_HALO + s * SUB, SUB), :] = section(0)
        gbbuf[rows, :] = section(1)
        urows = pl.ds(CONV_HALO + s * SUB, SUB)
        ubuf[urows, :] = section(2)
        ubuf[urows, :] = ubuf[urows, :] * section(3)

    def pool_mixer(s):
        rows = pl.ds(s * SUB, SUB)
        v0 = POOL_HALO + s * SUB
        pos = (i % tiles_per_seq) * TM + s * SUB + lax.broadcasted_iota(jnp.int32, (SUB, 1), 0)
        for g, w in enumerate(POOL_WINDOWS):
            cols = slice(g * POOL_GROUP_DIM, (g + 1) * POOL_GROUP_DIM)
            cur = vbuf[pl.ds(v0, SUB), cols]
            tot = cur
            for j in range(1, w):
                tot = tot + vbuf[pl.ds(v0 - j, SUB), cols]
            cnt = jnp.minimum(pos + 1, w).astype(f32)
            ymix[rows, cols] = (tot / cnt - cur).astype(bf16)

    def conv_mixer(s):
        rows = pl.ds(s * SUB, SUB)
        u0 = CONV_HALO + s * SUB
        conv = ubuf[pl.ds(u0, SUB), :] * conv_w_ref[CONV_K - 1:CONV_K, :]
        for k in range(CONV_K - 1):
            lag = CONV_K - 1 - k
            conv = conv + ubuf[pl.ds(u0 - lag, SUB), :] * conv_w_ref[k:k + 1, :]
        ymix[rows, POOL_WIDTH:] = (gbbuf[rows, :] * conv).astype(bf16)

    def out_proj(s):
        rows = pl.ds(s * SUB, SUB)
        o_ref[rows, :] = x_ref[rows, :] + jnp.dot(ymix[rows, :], w_out_ref[...],
                                                  preferred_element_type=f32)

    def ffn():
        for s in range(SUBTILES):
            rows = pl.ds(s * SUB, SUB)
            hn2buf[rows, :] = _rmsnorm(o_ref[rows, :], g2_ref[...]).astype(bf16)
        n_chunks = D_FF // FF_CHUNK
        for c in range(n_chunks):
            cols = slice(c * FF_CHUNK, (c + 1) * FF_CHUNK)
            gate = jnp.dot(hn2buf[...], w_gate_ref[:, cols], preferred_element_type=f32)
            up = jnp.dot(hn2buf[...], w_up_ref[:, cols], preferred_element_type=f32)
            a = (gate * jax.nn.sigmoid(gate) * up).astype(bf16)
            if c < n_chunks - 1:
                o_ref[...] += jnp.dot(a, w_down_ref[cols, :], preferred_element_type=f32)
            else:
                for r0 in range(0, TM, FINAL_ROWS):
                    rows = pl.ds(r0, FINAL_ROWS)
                    y = o_ref[rows, :] + jnp.dot(a[r0:r0 + FINAL_ROWS], w_down_ref[cols, :],
                                                 preferred_element_type=f32)
                    o_ref[rows, :] = _rmsnorm(y, gf_ref[...])

    for s in range(SUBTILES):
        in_proj(s)
    for s in range(SUBTILES):
        pool_mixer(s)
    for s in range(SUBTILES):
        conv_mixer(s)
        out_proj(s)
    vbuf[0:POOL_HALO, :] = vbuf[TM:TM + POOL_HALO, :]
    ubuf[0:CONV_HALO, :] = ubuf[TM:TM + CONV_HALO, :]
    ffn()


def kernel(x, norm1_g, w_in, pool_w, pool_scale, conv_w, w_out, norm2_g, w_gate, w_up, w_down, normf_g):
    batch, seq, d = x.shape
    assert d == D_MODEL and seq % TM == 0
    n_tok = batch * seq
    bf16 = jnp.bfloat16
    n_groups = len(POOL_WINDOWS)

    def resident(shape):
        return pl.BlockSpec(shape, lambda i: (0,) * len(shape), pipeline_mode=pl.Buffered(1))

    hbm = pl.BlockSpec(memory_space=pl.ANY)
    tile = pl.BlockSpec((TM, D_MODEL), lambda i: (i, 0))
    out = pl.pallas_call(
        functools.partial(_block_kernel, seq // TM),
        grid=(n_tok // TM,),
        in_specs=[
            tile,
            resident((1, D_MODEL)),
            hbm,
            hbm,
            resident((1, POOL_WIDTH)),
            resident((CONV_K, CONV_WIDTH)),
            hbm,
            resident((1, D_MODEL)),
            hbm,
            hbm,
            hbm,
            resident((1, D_MODEL)),
        ],
        out_specs=tile,
        out_shape=jax.ShapeDtypeStruct((n_tok, D_MODEL), x.dtype),
        scratch_shapes=[
            pltpu.VMEM((D_MODEL, IN_COLS), bf16),
            pltpu.VMEM((MIX_WIDTH, D_MODEL), bf16),
            pltpu.VMEM((D_MODEL, D_FF), bf16),
            pltpu.VMEM((D_MODEL, D_FF), bf16),
            pltpu.VMEM((D_FF, D_MODEL), bf16),
            pltpu.VMEM((STAGE_SLOTS, STAGE_ROWS, STAGE_COLS), jnp.float32),
            pltpu.VMEM((n_groups, POOL_GROUP_DIM, POOL_GROUP_DIM), jnp.float32),
            pltpu.SemaphoreType.DMA((STAGE_SLOTS,)),
            pltpu.SemaphoreType.DMA(()),
            pltpu.VMEM((POOL_HALO + TM, POOL_WIDTH), jnp.float32),
            pltpu.VMEM((CONV_HALO + TM, CONV_WIDTH), jnp.float32),
            pltpu.VMEM((TM, CONV_WIDTH), jnp.float32),
            pltpu.VMEM((TM, MIX_WIDTH), bf16),
            pltpu.VMEM((TM, D_MODEL), bf16),
        ],
        compiler_params=pltpu.CompilerParams(
            dimension_semantics=("arbitrary",),
            vmem_limit_bytes=VMEM_LIMIT_BYTES,
        ),
        name="hybrid_block",
    )(
        x.reshape(n_tok, D_MODEL),
        norm1_g.reshape(1, D_MODEL),
        w_in,
        pool_w,
        pool_scale.reshape(1, POOL_WIDTH),
        conv_w,
        w_out,
        norm2_g.reshape(1, D_MODEL),
        w_gate,
        w_up,
        w_down,
        normf_g.reshape(1, D_MODEL),
    )
    return out.reshape(batch, seq, D_MODEL)
```

```python
import functools

import jax
import jax.numpy as jnp
from jax import lax
from jax.experimental import pallas as pl
from jax.experimental.pallas import tpu as pltpu

D_MODEL = 1024
POOL_WIDTH = 512
CONV_WIDTH = 512
MIX_WIDTH = POOL_WIDTH + CONV_WIDTH
POOL_WINDOWS = (2, 4, 8, 16)
POOL_GROUP_DIM = 128
CONV_K = 3
IN_COLS = POOL_WIDTH + 3 * CONV_WIDTH
D_FF = 2816
RMS_EPS = 1e-6

SUB = 512
SUBTILES = 2
TM = SUB * SUBTILES
POOL_HALO = 16
assert all(w & (w - 1) == 0 and w <= POOL_HALO for w in POOL_WINDOWS)
CONV_HALO = 8
FF_CHUNK = 256
FINAL_ROWS = 256
STAGE_ROWS = 1024
STAGE_COLS = 512
STAGE_SLOTS = 3
VMEM_LIMIT_BYTES = 60 * 1024 * 1024


def _rmsnorm(x, g):
    ms = jnp.mean(x * x, axis=-1, keepdims=True)
    return (x * lax.rsqrt(ms + RMS_EPS)) * g


def _weight_chunks(w_in, w_out, w_gate, w_up, w_down, wb_in, wb_out, wb_gate, wb_up, wb_down):
    assert STAGE_COLS == POOL_WIDTH and STAGE_ROWS >= D_MODEL
    chunks = []
    for src, dst in ((w_in, wb_in), (w_out, wb_out), (w_gate, wb_gate), (w_up, wb_up), (w_down, wb_down)):
        n_rows, n_cols = src.shape
        for r0 in range(0, n_rows, STAGE_ROWS):
            rows = min(STAGE_ROWS, n_rows - r0)
            for c0 in range(0, n_cols, STAGE_COLS):
                cols = min(STAGE_COLS, n_cols - c0)
                idx = (pl.ds(r0, rows), pl.ds(c0, cols))
                chunks.append((src.at[idx], dst.at[idx], rows, cols, src is w_in and c0 == 0))
    return chunks


def _load_weights(w_in, pool_w, pool_scale_ref, w_out, w_gate, w_up, w_down,
                  wb_in, wb_out, wb_gate, wb_up, wb_down,
                  stage, pool_stage, sem, pool_sem):
    pool_copy = pltpu.make_async_copy(pool_w, pool_stage, pool_sem)
    pool_copy.start()
    chunks = _weight_chunks(w_in, w_out, w_gate, w_up, w_down, wb_in, wb_out, wb_gate, wb_up, wb_down)

    def copy(k):
        src, _, rows, cols, _ = chunks[k]
        slot = k % STAGE_SLOTS
        return pltpu.make_async_copy(src, stage.at[slot, pl.ds(0, rows), pl.ds(0, cols)], sem.at[slot])

    for k in range(min(STAGE_SLOTS - 1, len(chunks))):
        copy(k).start()
    for k, (_, dst, rows, cols, is_pool_section) in enumerate(chunks):
        if k + STAGE_SLOTS - 1 < len(chunks):
            copy(k + STAGE_SLOTS - 1).start()
        copy(k).wait()
        staged = stage.at[k % STAGE_SLOTS]
        if is_pool_section:
            pool_copy.wait()
            for g in range(len(POOL_WINDOWS)):
                gcols = slice(g * POOL_GROUP_DIM, (g + 1) * POOL_GROUP_DIM)
                group_map = pool_stage[g] * pool_scale_ref[:, gcols]
                dst[:, gcols] = jnp.dot(staged[0:rows, gcols], group_map,
                                        preferred_element_type=jnp.float32,
                                        precision=lax.Precision.HIGHEST).astype(jnp.bfloat16)
        else:
            dst[...] = staged[0:rows, 0:cols].astype(jnp.bfloat16)


def _block_kernel(tiles_per_seq,
                  x_ref, g1_ref, w_in_hbm, pool_w_hbm, pool_scale_ref, conv_w_ref,
                  w_out_hbm, g2_ref, w_gate_hbm, w_up_hbm, w_down_hbm, gf_ref,
                  o_ref,
                  w_in_ref, w_out_ref, w_gate_ref, w_up_ref, w_down_ref,
                  stage, pool_stage, sem, pool_sem,
                  vbuf, ubuf, gbbuf, ymix, hn2buf):
    i = pl.program_id(0)
    f32 = jnp.float32
    bf16 = jnp.bfloat16

    @pl.when(i == 0)
    def _():
        _load_weights(w_in_hbm, pool_w_hbm, pool_scale_ref, w_out_hbm, w_gate_hbm, w_up_hbm, w_down_hbm,
                      w_in_ref, w_out_ref, w_gate_ref, w_up_ref, w_down_ref,
                      stage, pool_stage, sem, pool_sem)

    @pl.when(i % tiles_per_seq == 0)
    def _():
        vbuf[0:POOL_HALO, :] = jnp.zeros((POOL_HALO, POOL_WIDTH), f32)
        ubuf[0:CONV_HALO, :] = jnp.zeros((CONV_HALO, CONV_WIDTH), f32)

    def in_proj(s):
        rows = pl.ds(s * SUB, SUB)
        hn = _rmsnorm(x_ref[rows, :], g1_ref[...]).astype(bf16)

        def section(k):
            cols = slice(k * POOL_WIDTH, (k + 1) * POOL_WIDTH)
            return jnp.dot(hn, w_in_ref[:, cols], preferred_element_type=f32)

        vbuf[pl.ds(POOL_HALO + s * SUB, SUB), :] = section(0)
        gbbuf[rows, :] = section(1)
        urows = pl.ds(CONV_HALO + s * SUB, SUB)
        ubuf[urows, :] = section(2)
        ubuf[urows, :] = ubuf[urows, :] * section(3)

    def pool_mixer(s):
        rows = pl.ds(s * SUB, SUB)
        pos = (i % tiles_per_seq) * TM + s * SUB + lax.broadcasted_iota(jnp.int32, (SUB, 1), 0)
        for g, w in enumerate(POOL_WINDOWS):
            cols = slice(g * POOL_GROUP_DIM, (g + 1) * POOL_GROUP_DIM)
            ext = vbuf[pl.ds(s * SUB, POOL_HALO + SUB), cols]
            tot = ext
            for k in range(w.bit_length() - 1):
                tot = tot + pltpu.roll(tot, 1 << k, 0)
            cnt = jnp.minimum(pos + 1, w).astype(f32)
            ymix[rows, cols] = (tot[POOL_HALO:] / cnt - ext[POOL_HALO:]).astype(bf16)

    def conv_mixer(s):
        rows = pl.ds(s * SUB, SUB)
        u0 = CONV_HALO + s * SUB
        conv = ubuf[pl.ds(u0, SUB), :] * conv_w_ref[CONV_K - 1:CONV_K, :]
        for k in range(CONV_K - 1):
            lag = CONV_K - 1 - k
            conv = conv + ubuf[pl.ds(u0 - lag, SUB), :] * conv_w_ref[k:k + 1, :]
        ymix[rows, POOL_WIDTH:] = (gbbuf[rows, :] * conv).astype(bf16)

    def out_proj(s):
        rows = pl.ds(s * SUB, SUB)
        o_ref[rows, :] = x_ref[rows, :] + jnp.dot(ymix[rows, :], w_out_ref[...],
                                                  preferred_element_type=f32)

    def ffn():
        for s in range(SUBTILES):
            rows = pl.ds(s * SUB, SUB)
            hn2buf[rows, :] = _rmsnorm(o_ref[rows, :], g2_ref[...]).astype(bf16)
        n_chunks = D_FF // FF_CHUNK
        for c in range(n_chunks):
            cols = slice(c * FF_CHUNK, (c + 1) * FF_CHUNK)
            gate = jnp.dot(hn2buf[...], w_gate_ref[:, cols], preferred_element_type=f32)
            up = jnp.dot(hn2buf[...], w_up_ref[:, cols], preferred_element_type=f32)
            a = (gate * jax.nn.sigmoid(gate) * up).astype(bf16)
            if c < n_chunks - 1:
                o_ref[...] += jnp.dot(a, w_down_ref[cols, :], preferred_element_type=f32)
            else:
                for r0 in range(0, TM, FINAL_ROWS):
                    rows = pl.ds(r0, FINAL_ROWS)
                    y = o_ref[rows, :] + jnp.dot(a[r0:r0 + FINAL_ROWS], w_down_ref[cols, :],
                                                 preferred_element_type=f32)
                    o_ref[rows, :] = _rmsnorm(y, gf_ref[...])

    for s in range(SUBTILES):
        in_proj(s)
    for s in range(SUBTILES):
        pool_mixer(s)
    for s in range(SUBTILES):
        conv_mixer(s)
        out_proj(s)
    vbuf[0:POOL_HALO, :] = vbuf[TM:TM + POOL_HALO, :]
    ubuf[0:CONV_HALO, :] = ubuf[TM:TM + CONV_HALO, :]
    ffn()


def kernel(x, norm1_g, w_in, pool_w, pool_scale, conv_w, w_out, norm2_g, w_gate, w_up, w_down, normf_g):
    batch, seq, d = x.shape
    assert d == D_MODEL and seq % TM == 0
    n_tok = batch * seq
    bf16 = jnp.bfloat16
    n_groups = len(POOL_WINDOWS)

    def resident(shape):
        return pl.BlockSpec(shape, lambda i: (0,) * len(shape), pipeline_mode=pl.Buffered(1))

    hbm = pl.BlockSpec(memory_space=pl.ANY)
    tile = pl.BlockSpec((TM, D_MODEL), lambda i: (i, 0))
    out = pl.pallas_call(
        functools.partial(_block_kernel, seq // TM),
        grid=(n_tok // TM,),
        in_specs=[
            tile,
            resident((1, D_MODEL)),
            hbm,
            hbm,
            resident((1, POOL_WIDTH)),
            resident((CONV_K, CONV_WIDTH)),
            hbm,
            resident((1, D_MODEL)),
            hbm,
            hbm,
            hbm,
            resident((1, D_MODEL)),
        ],
        out_specs=tile,
        out_shape=jax.ShapeDtypeStruct((n_tok, D_MODEL), x.dtype),
        scratch_shapes=[
            pltpu.VMEM((D_MODEL, IN_COLS), bf16),
            pltpu.VMEM((MIX_WIDTH, D_MODEL), bf16),
            pltpu.VMEM((D_MODEL, D_FF), bf16),
            pltpu.VMEM((D_MODEL, D_FF), bf16),
            pltpu.VMEM((D_FF, D_MODEL), bf16),
            pltpu.VMEM((STAGE_SLOTS, STAGE_ROWS, STAGE_COLS), jnp.float32),
            pltpu.VMEM((n_groups, POOL_GROUP_DIM, POOL_GROUP_DIM), jnp.float32),
            pltpu.SemaphoreType.DMA((STAGE_SLOTS,)),
            pltpu.SemaphoreType.DMA(()),
            pltpu.VMEM((POOL_HALO + TM, POOL_WIDTH), jnp.float32),
            pltpu.VMEM((CONV_HALO + TM, CONV_WIDTH), jnp.float32),
            pltpu.VMEM((TM, CONV_WIDTH), jnp.float32),
            pltpu.VMEM((TM, MIX_WIDTH), bf16),
            pltpu.VMEM((TM, D_MODEL), bf16),
        ],
        compiler_params=pltpu.CompilerParams(
            dimension_semantics=("arbitrary",),
            vmem_limit_bytes=VMEM_LIMIT_BYTES,
        ),
        name="hybrid_block",
    )(
        x.reshape(n_tok, D_MODEL),
        norm1_g.reshape(1, D_MODEL),
        w_in,
        pool_w,
        pool_scale.reshape(1, POOL_WIDTH),
        conv_w,
        w_out,
        norm2_g.reshape(1, D_MODEL),
        w_gate,
        w_up,
        w_down,
        normf_g.reshape(1, D_MODEL),
    )
    return out.reshape(batch, seq, D_MODEL)
```

```python
import functools

import jax
import jax.numpy as jnp
from jax import lax
from jax.experimental import pallas as pl
from jax.experimental.pallas import tpu as pltpu

D_MODEL = 1024
POOL_WIDTH = 512
CONV_WIDTH = 512
MIX_WIDTH = POOL_WIDTH + CONV_WIDTH
POOL_WINDOWS = (2, 4, 8, 16)
POOL_GROUP_DIM = 128
CONV_K = 3
IN_COLS = POOL_WIDTH + 3 * CONV_WIDTH
D_FF = 2816
RMS_EPS = 1e-6

SUB = 256
SUBTILES = 4
TM = SUB * SUBTILES
POOL_HALO = 16
assert all(w & (w - 1) == 0 and w <= POOL_HALO for w in POOL_WINDOWS)
CONV_HALO = 8
FF_CHUNK = 256
FINAL_ROWS = 256
STAGE_ROWS = 1024
STAGE_COLS = 512
STAGE_SLOTS = 3
VMEM_LIMIT_BYTES = 60 * 1024 * 1024


def _rmsnorm(x, g):
    ms = jnp.mean(x * x, axis=-1, keepdims=True)
    return (x * lax.rsqrt(ms + RMS_EPS)) * g


def _weight_chunks(w_in, w_out, w_gate, w_up, w_down, wb_in, wb_out, wb_gate, wb_up, wb_down):
    assert STAGE_COLS == POOL_WIDTH and STAGE_ROWS >= D_MODEL
    chunks = []
    for src, dst in ((w_in, wb_in), (w_out, wb_out), (w_gate, wb_gate), (w_up, wb_up), (w_down, wb_down)):
        n_rows, n_cols = src.shape
        for r0 in range(0, n_rows, STAGE_ROWS):
            rows = min(STAGE_ROWS, n_rows - r0)
            for c0 in range(0, n_cols, STAGE_COLS):
                cols = min(STAGE_COLS, n_cols - c0)
                idx = (pl.ds(r0, rows), pl.ds(c0, cols))
                chunks.append((src.at[idx], dst.at[idx], rows, cols, src is w_in and c0 == 0))
    return chunks


def _load_weights(w_in, pool_w, pool_scale_ref, w_out, w_gate, w_up, w_down,
                  wb_in, wb_out, wb_gate, wb_up, wb_down,
                  stage, pool_stage, sem, pool_sem):
    pool_copy = pltpu.make_async_copy(pool_w, pool_stage, pool_sem)
    pool_copy.start()
    chunks = _weight_chunks(w_in, w_out, w_gate, w_up, w_down, wb_in, wb_out, wb_gate, wb_up, wb_down)

    def copy(k):
        src, _, rows, cols, _ = chunks[k]
        slot = k % STAGE_SLOTS
        return pltpu.make_async_copy(src, stage.at[slot, pl.ds(0, rows), pl.ds(0, cols)], sem.at[slot])

    for k in range(min(STAGE_SLOTS - 1, len(chunks))):
        copy(k).start()
    for k, (_, dst, rows, cols, is_pool_section) in enumerate(chunks):
        if k + STAGE_SLOTS - 1 < len(chunks):
            copy(k + STAGE_SLOTS - 1).start()
        copy(k).wait()
        staged = stage.at[k % STAGE_SLOTS]
        if is_pool_section:
            pool_copy.wait()
            for g in range(len(POOL_WINDOWS)):
                gcols = slice(g * POOL_GROUP_DIM, (g + 1) * POOL_GROUP_DIM)
                group_map = pool_stage[g] * pool_scale_ref[:, gcols]
                dst[:, gcols] = jnp.dot(staged[0:rows, gcols], group_map,
                                        preferred_element_type=jnp.float32,
                                        precision=lax.Precision.HIGHEST).astype(jnp.bfloat16)
        else:
            dst[...] = staged[0:rows, 0:cols].astype(jnp.bfloat16)


def _block_kernel(tiles_per_seq,
                  x_ref, g1_ref, w_in_hbm, pool_w_hbm, pool_scale_ref, conv_w_ref,
                  w_out_hbm, g2_ref, w_gate_hbm, w_up_hbm, w_down_hbm, gf_ref,
                  o_ref,
                  w_in_ref, w_out_ref, w_gate_ref, w_up_ref, w_down_ref,
                  stage, pool_stage, sem, pool_sem,
                  vbuf, ubuf, gbbuf, ymix, hn2buf):
    i = pl.program_id(0)
    f32 = jnp.float32
    bf16 = jnp.bfloat16

    @pl.when(i == 0)
    def _():
        _load_weights(w_in_hbm, pool_w_hbm, pool_scale_ref, w_out_hbm, w_gate_hbm, w_up_hbm, w_down_hbm,
                      w_in_ref, w_out_ref, w_gate_ref, w_up_ref, w_down_ref,
                      stage, pool_stage, sem, pool_sem)

    @pl.when(i % tiles_per_seq == 0)
    def _():
        vbuf[0:POOL_HALO, :] = jnp.zeros((POOL_HALO, POOL_WIDTH), f32)
        ubuf[0:CONV_HALO, :] = jnp.zeros((CONV_HALO, CONV_WIDTH), f32)

    def in_proj(s):
        rows = pl.ds(s * SUB, SUB)
        hn = _rmsnorm(x_ref[rows, :], g1_ref[...]).astype(bf16)

        def section(k):
            cols = slice(k * POOL_WIDTH, (k + 1) * POOL_WIDTH)
            return jnp.dot(hn, w_in_ref[:, cols], preferred_element_type=f32)

        vbuf[pl.ds(POOL_HALO + s * SUB, SUB), :] = section(0)
        gbbuf[rows, :] = section(1)
        urows = pl.ds(CONV_HALO + s * SUB, SUB)
        ubuf[urows, :] = section(2)
        ubuf[urows, :] = ubuf[urows, :] * section(3)

    def pool_mixer(s):
        rows = pl.ds(s * SUB, SUB)
        pos = (i % tiles_per_seq) * TM + s * SUB + lax.broadcasted_iota(jnp.int32, (SUB, 1), 0)
        n_seen = (pos + 1).astype(f32)
        for g, w in enumerate(POOL_WINDOWS):
            cols = slice(g * POOL_GROUP_DIM, (g + 1) * POOL_GROUP_DIM)
            ext = vbuf[pl.ds(s * SUB, POOL_HALO + SUB), cols]
            tot = ext
            for k in range(w.bit_length() - 1):
                tot = tot + pltpu.roll(tot, 1 << k, 0)
            inv_cnt = 1.0 / jnp.minimum(n_seen, float(w))
            ymix[rows, cols] = (tot[POOL_HALO:] * inv_cnt - ext[POOL_HALO:]).astype(bf16)

    def conv_mixer(s):
        rows = pl.ds(s * SUB, SUB)
        u0 = CONV_HALO + s * SUB
        conv = ubuf[pl.ds(u0, SUB), :] * conv_w_ref[CONV_K - 1:CONV_K, :]
        for k in range(CONV_K - 1):
            lag = CONV_K - 1 - k
            conv = conv + ubuf[pl.ds(u0 - lag, SUB), :] * conv_w_ref[k:k + 1, :]
        ymix[rows, POOL_WIDTH:] = (gbbuf[rows, :] * conv).astype(bf16)

    def out_proj(s):
        rows = pl.ds(s * SUB, SUB)
        o_ref[rows, :] = x_ref[rows, :] + jnp.dot(ymix[rows, :], w_out_ref[...],
                                                  preferred_element_type=f32)

    def ffn():
        for s in range(SUBTILES):
            rows = pl.ds(s * SUB, SUB)
            hn2buf[rows, :] = _rmsnorm(o_ref[rows, :], g2_ref[...]).astype(bf16)
        n_chunks = D_FF // FF_CHUNK
        for c in range(n_chunks):
            cols = slice(c * FF_CHUNK, (c + 1) * FF_CHUNK)
            gate = jnp.dot(hn2buf[...], w_gate_ref[:, cols], preferred_element_type=f32)
            up = jnp.dot(hn2buf[...], w_up_ref[:, cols], preferred_element_type=f32)
            a = (gate * jax.nn.sigmoid(gate) * up).astype(bf16)
            if c < n_chunks - 1:
                o_ref[...] += jnp.dot(a, w_down_ref[cols, :], preferred_element_type=f32)
            else:
                for r0 in range(0, TM, FINAL_ROWS):
                    rows = pl.ds(r0, FINAL_ROWS)
                    y = o_ref[rows, :] + jnp.dot(a[r0:r0 + FINAL_ROWS], w_down_ref[cols, :],
                                                 preferred_element_type=f32)
                    o_ref[rows, :] = _rmsnorm(y, gf_ref[...])

    for s in range(SUBTILES):
        in_proj(s)
    for s in range(SUBTILES):
        pool_mixer(s)
        conv_mixer(s)
        out_proj(s)
    vbuf[0:POOL_HALO, :] = vbuf[TM:TM + POOL_HALO, :]
    ubuf[0:CONV_HALO, :] = ubuf[TM:TM + CONV_HALO, :]
    ffn()


def kernel(x, norm1_g, w_in, pool_w, pool_scale, conv_w, w_out, norm2_g, w_gate, w_up, w_down, normf_g):
    batch, seq, d = x.shape
    assert d == D_MODEL and seq % TM == 0
    n_tok = batch * seq
    bf16 = jnp.bfloat16
    n_groups = len(POOL_WINDOWS)

    def resident(shape):
        return pl.BlockSpec(shape, lambda i: (0,) * len(shape), pipeline_mode=pl.Buffered(1))

    hbm = pl.BlockSpec(memory_space=pl.ANY)
    tile = pl.BlockSpec((TM, D_MODEL), lambda i: (i, 0))
    out = pl.pallas_call(
        functools.partial(_block_kernel, seq // TM),
        grid=(n_tok // TM,),
        in_specs=[
            tile,
            resident((1, D_MODEL)),
            hbm,
            hbm,
            resident((1, POOL_WIDTH)),
            resident((CONV_K, CONV_WIDTH)),
            hbm,
            resident((1, D_MODEL)),
            hbm,
            hbm,
            hbm,
            resident((1, D_MODEL)),
        ],
        out_specs=tile,
        out_shape=jax.ShapeDtypeStruct((n_tok, D_MODEL), x.dtype),
        scratch_shapes=[
            pltpu.VMEM((D_MODEL, IN_COLS), bf16),
            pltpu.VMEM((MIX_WIDTH, D_MODEL), bf16),
            pltpu.VMEM((D_MODEL, D_FF), bf16),
            pltpu.VMEM((D_MODEL, D_FF), bf16),
            pltpu.VMEM((D_FF, D_MODEL), bf16),
            pltpu.VMEM((STAGE_SLOTS, STAGE_ROWS, STAGE_COLS), jnp.float32),
            pltpu.VMEM((n_groups, POOL_GROUP_DIM, POOL_GROUP_DIM), jnp.float32),
            pltpu.SemaphoreType.DMA((STAGE_SLOTS,)),
            pltpu.SemaphoreType.DMA(()),
            pltpu.VMEM((POOL_HALO + TM, POOL_WIDTH), jnp.float32),
            pltpu.VMEM((CONV_HALO + TM, CONV_WIDTH), jnp.float32),
            pltpu.VMEM((TM, CONV_WIDTH), jnp.float32),
            pltpu.VMEM((TM, MIX_WIDTH), bf16),
            pltpu.VMEM((TM, D_MODEL), bf16),
        ],
        compiler_params=pltpu.CompilerParams(
            dimension_semantics=("arbitrary",),
            vmem_limit_bytes=VMEM_LIMIT_BYTES,
        ),
        name="hybrid_block",
    )(
        x.reshape(n_tok, D_MODEL),
        norm1_g.reshape(1, D_MODEL),
        w_in,
        pool_w,
        pool_scale.reshape(1, POOL_WIDTH),
        conv_w,
        w_out,
        norm2_g.reshape(1, D_MODEL),
        w_gate,
        w_up,
        w_down,
        normf_g.reshape(1, D_MODEL),
    )
    return out.reshape(batch, seq, D_MODEL)
```

```python
import functools

import jax
import jax.numpy as jnp
from jax import lax
from jax.experimental import pallas as pl
from jax.experimental.pallas import tpu as pltpu

D_MODEL = 1024
POOL_WIDTH = 512
CONV_WIDTH = 512
MIX_WIDTH = POOL_WIDTH + CONV_WIDTH
POOL_WINDOWS = (2, 4, 8, 16)
POOL_GROUP_DIM = 128
CONV_K = 3
IN_COLS = POOL_WIDTH + 3 * CONV_WIDTH
D_FF = 2816
RMS_EPS = 1e-6

SUB = 256
SUBTILES = 4
TM = SUB * SUBTILES
POOL_HALO = 16
assert all(w & (w - 1) == 0 and w <= POOL_HALO for w in POOL_WINDOWS)
CONV_HALO = 8
FF_CHUNK = 256
FINAL_ROWS = 256
STAGE_ROWS = 1024
STAGE_COLS = 512
STAGE_SLOTS = 4
VMEM_LIMIT_BYTES = 60 * 1024 * 1024


def _rmsnorm(x, g):
    ms = jnp.mean(x * x, axis=-1, keepdims=True)
    return (x * lax.rsqrt(ms + RMS_EPS)) * g


def _weight_chunks(w_in, w_out, w_gate, w_up, w_down, wb_in, wb_out, wb_gate, wb_up, wb_down):
    assert STAGE_COLS == POOL_WIDTH and STAGE_ROWS >= D_MODEL
    chunks = []
    for src, dst in ((w_in, wb_in), (w_out, wb_out), (w_gate, wb_gate), (w_up, wb_up), (w_down, wb_down)):
        n_rows, n_cols = src.shape
        for r0 in range(0, n_rows, STAGE_ROWS):
            rows = min(STAGE_ROWS, n_rows - r0)
            for c0 in range(0, n_cols, STAGE_COLS):
                cols = min(STAGE_COLS, n_cols - c0)
                idx = (pl.ds(r0, rows), pl.ds(c0, cols))
                chunks.append((src.at[idx], dst.at[idx], rows, cols, src is w_in and c0 == 0))
    return chunks


def _load_weights(w_in, pool_w, pool_scale_ref, w_out, w_gate, w_up, w_down,
                  wb_in, wb_out, wb_gate, wb_up, wb_down,
                  stage, pool_stage, sem, pool_sem):
    pool_copy = pltpu.make_async_copy(pool_w, pool_stage, pool_sem)
    pool_copy.start()
    chunks = _weight_chunks(w_in, w_out, w_gate, w_up, w_down, wb_in, wb_out, wb_gate, wb_up, wb_down)

    def copy(k):
        src, _, rows, cols, _ = chunks[k]
        slot = k % STAGE_SLOTS
        return pltpu.make_async_copy(src, stage.at[slot, pl.ds(0, rows), pl.ds(0, cols)], sem.at[slot])

    for k in range(min(STAGE_SLOTS - 1, len(chunks))):
        copy(k).start()
    for k, (_, dst, rows, cols, is_pool_section) in enumerate(chunks):
        if k + STAGE_SLOTS - 1 < len(chunks):
            copy(k + STAGE_SLOTS - 1).start()
        copy(k).wait()
        staged = stage.at[k % STAGE_SLOTS]
        if is_pool_section:
            pool_copy.wait()
            for g in range(len(POOL_WINDOWS)):
                gcols = slice(g * POOL_GROUP_DIM, (g + 1) * POOL_GROUP_DIM)
                group_map = pool_stage[g] * pool_scale_ref[:, gcols]
                dst[:, gcols] = jnp.dot(staged[0:rows, gcols], group_map,
                                        preferred_element_type=jnp.float32,
                                        precision=lax.Precision.HIGHEST).astype(jnp.bfloat16)
        else:
            dst[...] = staged[0:rows, 0:cols].astype(jnp.bfloat16)


def _block_kernel(tiles_per_seq,
                  x_ref, g1_ref, w_in_hbm, pool_w_hbm, pool_scale_ref, conv_w_ref,
                  w_out_hbm, g2_ref, w_gate_hbm, w_up_hbm, w_down_hbm, gf_ref,
                  o_ref,
                  w_in_ref, w_out_ref, w_gate_ref, w_up_ref, w_down_ref,
                  stage, pool_stage, sem, pool_sem,
                  vbuf, ubuf, gbbuf, ymix, hn2buf):
    i = pl.program_id(0)
    f32 = jnp.float32
    bf16 = jnp.bfloat16

    @pl.when(i == 0)
    def _():
        _load_weights(w_in_hbm, pool_w_hbm, pool_scale_ref, w_out_hbm, w_gate_hbm, w_up_hbm, w_down_hbm,
                      w_in_ref, w_out_ref, w_gate_ref, w_up_ref, w_down_ref,
                      stage, pool_stage, sem, pool_sem)

    @pl.when(i % tiles_per_seq == 0)
    def _():
        vbuf[0:POOL_HALO, :] = jnp.zeros((POOL_HALO, POOL_WIDTH), f32)
        ubuf[0:CONV_HALO, :] = jnp.zeros((CONV_HALO, CONV_WIDTH), f32)

    def in_proj(s):
        rows = pl.ds(s * SUB, SUB)
        hn = _rmsnorm(x_ref[rows, :], g1_ref[...]).astype(bf16)

        def section(k):
            cols = slice(k * POOL_WIDTH, (k + 1) * POOL_WIDTH)
            return jnp.dot(hn, w_in_ref[:, cols], preferred_element_type=f32)

        vbuf[pl.ds(POOL_HALO + s * SUB, SUB), :] = section(0)
        gbbuf[rows, :] = section(1)
        urows = pl.ds(CONV_HALO + s * SUB, SUB)
        ubuf[urows, :] = section(2)
        ubuf[urows, :] = ubuf[urows, :] * section(3)

    def pool_mixer(s):
        rows = pl.ds(s * SUB, SUB)
        pos = (i % tiles_per_seq) * TM + s * SUB + lax.broadcasted_iota(jnp.int32, (SUB, 1), 0)
        n_seen = (pos + 1).astype(f32)
        for g, w in enumerate(POOL_WINDOWS):
            cols = slice(g * POOL_GROUP_DIM, (g + 1) * POOL_GROUP_DIM)
            ext = vbuf[pl.ds(s * SUB, POOL_HALO + SUB), cols]
            tot = ext
            for k in range(w.bit_length() - 1):
                tot = tot + pltpu.roll(tot, 1 << k, 0)
            inv_cnt = 1.0 / jnp.minimum(n_seen, float(w))
            ymix[rows, cols] = (tot[POOL_HALO:] * inv_cnt - ext[POOL_HALO:]).astype(bf16)

    def conv_mixer(s):
        rows = pl.ds(s * SUB, SUB)
        u0 = CONV_HALO + s * SUB
        conv = ubuf[pl.ds(u0, SUB), :] * conv_w_ref[CONV_K - 1:CONV_K, :]
        for k in range(CONV_K - 1):
            lag = CONV_K - 1 - k
            conv = conv + ubuf[pl.ds(u0 - lag, SUB), :] * conv_w_ref[k:k + 1, :]
        ymix[rows, POOL_WIDTH:] = (gbbuf[rows, :] * conv).astype(bf16)

    def out_proj(s):
        rows = pl.ds(s * SUB, SUB)
        o_ref[rows, :] = x_ref[rows, :] + jnp.dot(ymix[rows, :], w_out_ref[...],
                                                  preferred_element_type=f32)

    def ffn():
        for s in range(SUBTILES):
            rows = pl.ds(s * SUB, SUB)
            hn2buf[rows, :] = _rmsnorm(o_ref[rows, :], g2_ref[...]).astype(bf16)
        n_chunks = D_FF // FF_CHUNK
        for c in range(n_chunks):
            cols = slice(c * FF_CHUNK, (c + 1) * FF_CHUNK)
            gate = jnp.dot(hn2buf[...], w_gate_ref[:, cols], preferred_element_type=f32)
            up = jnp.dot(hn2buf[...], w_up_ref[:, cols], preferred_element_type=f32)
            a = (gate * jax.nn.sigmoid(gate) * up).astype(bf16)
            if c < n_chunks - 1:
                o_ref[...] += jnp.dot(a, w_down_ref[cols, :], preferred_element_type=f32)
            else:
                for r0 in range(0, TM, FINAL_ROWS):
                    rows = pl.ds(r0, FINAL_ROWS)
                    y = o_ref[rows, :] + jnp.dot(a[r0:r0 + FINAL_ROWS], w_down_ref[cols, :],
                                                 preferred_element_type=f32)
                    o_ref[rows, :] = _rmsnorm(y, gf_ref[...])

    for s in range(SUBTILES):
        in_proj(s)
    for s in range(SUBTILES):
        pool_mixer(s)
        conv_mixer(s)
        out_proj(s)
    vbuf[0:POOL_HALO, :] = vbuf[TM:TM + POOL_HALO, :]
    ubuf[0:CONV_HALO, :] = ubuf[TM:TM + CONV_HALO, :]
    ffn()


def kernel(x, norm1_g, w_in, pool_w, pool_scale, conv_w, w_out, norm2_g, w_gate, w_up, w_down, normf_g):
    batch, seq, d = x.shape
    assert d == D_MODEL and seq % TM == 0
    n_tok = batch * seq
    bf16 = jnp.bfloat16
    n_groups = len(POOL_WINDOWS)

    def resident(shape):
        return pl.BlockSpec(shape, lambda i: (0,) * len(shape), pipeline_mode=pl.Buffered(1))

    hbm = pl.BlockSpec(memory_space=pl.ANY)
    tile = pl.BlockSpec((TM, D_MODEL), lambda i: (i, 0))
    out = pl.pallas_call(
        functools.partial(_block_kernel, seq // TM),
        grid=(n_tok // TM,),
        in_specs=[
            tile,
            resident((1, D_MODEL)),
            hbm,
            hbm,
            resident((1, POOL_WIDTH)),
            resident((CONV_K, CONV_WIDTH)),
            hbm,
            resident((1, D_MODEL)),
            hbm,
            hbm,
            hbm,
            resident((1, D_MODEL)),
        ],
        out_specs=tile,
        out_shape=jax.ShapeDtypeStruct((n_tok, D_MODEL), x.dtype),
        scratch_shapes=[
            pltpu.VMEM((D_MODEL, IN_COLS), bf16),
            pltpu.VMEM((MIX_WIDTH, D_MODEL), bf16),
            pltpu.VMEM((D_MODEL, D_FF), bf16),
            pltpu.VMEM((D_MODEL, D_FF), bf16),
            pltpu.VMEM((D_FF, D_MODEL), bf16),
            pltpu.VMEM((STAGE_SLOTS, STAGE_ROWS, STAGE_COLS), jnp.float32),
            pltpu.VMEM((n_groups, POOL_GROUP_DIM, POOL_GROUP_DIM), jnp.float32),
            pltpu.SemaphoreType.DMA((STAGE_SLOTS,)),
            pltpu.SemaphoreType.DMA(()),
            pltpu.VMEM((POOL_HALO + TM, POOL_WIDTH), jnp.float32),
            pltpu.VMEM((CONV_HALO + TM, CONV_WIDTH), jnp.float32),
            pltpu.VMEM((TM, CONV_WIDTH), jnp.float32),
            pltpu.VMEM((TM, MIX_WIDTH), bf16),
            pltpu.VMEM((TM, D_MODEL), bf16),
        ],
        compiler_params=pltpu.CompilerParams(
            dimension_semantics=("arbitrary",),
            vmem_limit_bytes=VMEM_LIMIT_BYTES,
        ),
        name="hybrid_block",
    )(
        x.reshape(n_tok, D_MODEL),
        norm1_g.reshape(1, D_MODEL),
        w_in,
        pool_w,
        pool_scale.reshape(1, POOL_WIDTH),
        conv_w,
        w_out,
        norm2_g.reshape(1, D_MODEL),
        w_gate,
        w_up,
        w_down,
        normf_g.reshape(1, D_MODEL),
    )
    return out.reshape(batch, seq, D_MODEL)
```

```python
import functools

import jax
import jax.numpy as jnp
from jax import lax
from jax.experimental import pallas as pl
from jax.experimental.pallas import tpu as pltpu

D_MODEL = 1024
POOL_WIDTH = 512
CONV_WIDTH = 512
MIX_WIDTH = POOL_WIDTH + CONV_WIDTH
POOL_WINDOWS = (2, 4, 8, 16)
POOL_GROUP_DIM = 128
CONV_K = 3
IN_COLS = POOL_WIDTH + 3 * CONV_WIDTH
D_FF = 2816
RMS_EPS = 1e-6

SUB = 256
SUBTILES = 4
TM = SUB * SUBTILES
POOL_HALO = 16
assert all(w & (w - 1) == 0 and w <= POOL_HALO for w in POOL_WINDOWS)
CONV_HALO = 8
FF_CHUNK = 256
FINAL_ROWS = 256
STAGE_ROWS = 1024
STAGE_COLS = 512
STAGE_SLOTS = 4
VMEM_LIMIT_BYTES = 60 * 1024 * 1024


def _rmsnorm(x, g):
    ms = jnp.mean(x * x, axis=-1, keepdims=True)
    return (x * lax.rsqrt(ms + RMS_EPS)) * g


def _weight_chunks(w_in, w_out, w_gate, w_up, w_down, wb_in, wb_out, wb_gate, wb_up, wb_down):
    assert STAGE_COLS == POOL_WIDTH and STAGE_ROWS >= D_MODEL
    chunks = []
    for src, dst in ((w_in, wb_in), (w_out, wb_out), (w_gate, wb_gate), (w_up, wb_up), (w_down, wb_down)):
        n_rows, n_cols = src.shape
        for r0 in range(0, n_rows, STAGE_ROWS):
            rows = min(STAGE_ROWS, n_rows - r0)
            for c0 in range(0, n_cols, STAGE_COLS):
                cols = min(STAGE_COLS, n_cols - c0)
                idx = (pl.ds(r0, rows), pl.ds(c0, cols))
                chunks.append((src.at[idx], dst.at[idx], rows, cols, src is w_in and c0 == 0))
    return chunks


def _load_weights(w_in, pool_w, pool_scale_ref, w_out, w_gate, w_up, w_down,
                  wb_in, wb_out, wb_gate, wb_up, wb_down,
                  stage, pool_stage, sem, pool_sem):
    pool_copy = pltpu.make_async_copy(pool_w, pool_stage, pool_sem)
    pool_copy.start()
    chunks = _weight_chunks(w_in, w_out, w_gate, w_up, w_down, wb_in, wb_out, wb_gate, wb_up, wb_down)

    def copy(k):
        src, _, rows, cols, _ = chunks[k]
        slot = k % STAGE_SLOTS
        return pltpu.make_async_copy(src, stage.at[slot, pl.ds(0, rows), pl.ds(0, cols)], sem.at[slot])

    for k in range(min(STAGE_SLOTS - 1, len(chunks))):
        copy(k).start()
    for k, (_, dst, rows, cols, is_pool_section) in enumerate(chunks):
        if k + STAGE_SLOTS - 1 < len(chunks):
            copy(k + STAGE_SLOTS - 1).start()
        copy(k).wait()
        staged = stage.at[k % STAGE_SLOTS]
        if is_pool_section:
            pool_copy.wait()
            for g in range(len(POOL_WINDOWS)):
                gcols = slice(g * POOL_GROUP_DIM, (g + 1) * POOL_GROUP_DIM)
                group_map = pool_stage[g] * pool_scale_ref[:, gcols]
                dst[:, gcols] = jnp.dot(staged[0:rows, gcols], group_map,
                                        preferred_element_type=jnp.float32,
                                        precision=lax.Precision.HIGHEST).astype(jnp.bfloat16)
        else:
            dst[...] = staged[0:rows, 0:cols].astype(jnp.bfloat16)


def _block_kernel(tiles_per_seq,
                  x_ref, g1_ref, w_in_hbm, pool_w_hbm, pool_scale_ref, conv_w_ref,
                  w_out_hbm, g2_ref, w_gate_hbm, w_up_hbm, w_down_hbm, gf_ref,
                  o_ref,
                  w_in_ref, w_out_ref, w_gate_ref, w_up_ref, w_down_ref,
                  stage, pool_stage, sem, pool_sem,
                  vbuf, ubuf, gbbuf, ymix, hn2buf):
    i = pl.program_id(0)
    f32 = jnp.float32
    bf16 = jnp.bfloat16

    @pl.when(i == 0)
    def _():
        _load_weights(w_in_hbm, pool_w_hbm, pool_scale_ref, w_out_hbm, w_gate_hbm, w_up_hbm, w_down_hbm,
                      w_in_ref, w_out_ref, w_gate_ref, w_up_ref, w_down_ref,
                      stage, pool_stage, sem, pool_sem)

    @pl.when(i % tiles_per_seq == 0)
    def _():
        vbuf[0:POOL_HALO, :] = jnp.zeros((POOL_HALO, POOL_WIDTH), f32)
        ubuf[0:CONV_HALO, :] = jnp.zeros((CONV_HALO, CONV_WIDTH), f32)

    def in_proj():
        hn = jnp.concatenate(
            [_rmsnorm(x_ref[pl.ds(s * SUB, SUB), :], g1_ref[...]).astype(bf16) for s in range(SUBTILES)], axis=0)

        def section(k):
            cols = slice(k * POOL_WIDTH, (k + 1) * POOL_WIDTH)
            return jnp.dot(hn, w_in_ref[:, cols], preferred_element_type=f32)

        vbuf[pl.ds(POOL_HALO, TM), :] = section(0)
        urows = pl.ds(CONV_HALO, TM)
        ubuf[urows, :] = section(2)
        ubuf[urows, :] = ubuf[urows, :] * section(3)
        gbbuf[...] = section(1)

    def pool_mixer(s):
        rows = pl.ds(s * SUB, SUB)
        pos = (i % tiles_per_seq) * TM + s * SUB + lax.broadcasted_iota(jnp.int32, (SUB, 1), 0)
        n_seen = (pos + 1).astype(f32)
        for g, w in enumerate(POOL_WINDOWS):
            cols = slice(g * POOL_GROUP_DIM, (g + 1) * POOL_GROUP_DIM)
            ext = vbuf[pl.ds(s * SUB, POOL_HALO + SUB), cols]
            tot = ext
            for k in range(w.bit_length() - 1):
                tot = tot + pltpu.roll(tot, 1 << k, 0)
            inv_cnt = 1.0 / jnp.minimum(n_seen, float(w))
            ymix[rows, cols] = (tot[POOL_HALO:] * inv_cnt - ext[POOL_HALO:]).astype(bf16)

    def conv_mixer(s):
        rows = pl.ds(s * SUB, SUB)
        u0 = CONV_HALO + s * SUB
        conv = ubuf[pl.ds(u0, SUB), :] * conv_w_ref[CONV_K - 1:CONV_K, :]
        for k in range(CONV_K - 1):
            lag = CONV_K - 1 - k
            conv = conv + ubuf[pl.ds(u0 - lag, SUB), :] * conv_w_ref[k:k + 1, :]
        ymix[rows, POOL_WIDTH:] = (gbbuf[rows, :] * conv).astype(bf16)

    def out_proj(s):
        rows = pl.ds(s * SUB, SUB)
        o_ref[rows, :] = x_ref[rows, :] + jnp.dot(ymix[rows, :], w_out_ref[...],
                                                  preferred_element_type=f32)

    def ffn():
        for s in range(SUBTILES):
            rows = pl.ds(s * SUB, SUB)
            hn2buf[rows, :] = _rmsnorm(o_ref[rows, :], g2_ref[...]).astype(bf16)
        n_chunks = D_FF // FF_CHUNK
        for c in range(n_chunks):
            cols = slice(c * FF_CHUNK, (c + 1) * FF_CHUNK)
            gate = jnp.dot(hn2buf[...], w_gate_ref[:, cols], preferred_element_type=f32)
            up = jnp.dot(hn2buf[...], w_up_ref[:, cols], preferred_element_type=f32)
            a = (gate * jax.nn.sigmoid(gate) * up).astype(bf16)
            if c < n_chunks - 1:
                o_ref[...] += jnp.dot(a, w_down_ref[cols, :], preferred_element_type=f32)
            else:
                for r0 in range(0, TM, FINAL_ROWS):
                    rows = pl.ds(r0, FINAL_ROWS)
                    y = o_ref[rows, :] + jnp.dot(a[r0:r0 + FINAL_ROWS], w_down_ref[cols, :],
                                                 preferred_element_type=f32)
                    o_ref[rows, :] = _rmsnorm(y, gf_ref[...])

    in_proj()
    for s in range(SUBTILES):
        pool_mixer(s)
        conv_mixer(s)
        out_proj(s)
    vbuf[0:POOL_HALO, :] = vbuf[TM:TM + POOL_HALO, :]
    ubuf[0:CONV_HALO, :] = ubuf[TM:TM + CONV_HALO, :]
    ffn()


def kernel(x, norm1_g, w_in, pool_w, pool_scale, conv_w, w_out, norm2_g, w_gate, w_up, w_down, normf_g):
    batch, seq, d = x.shape
    assert d == D_MODEL and seq % TM == 0
    n_tok = batch * seq
    bf16 = jnp.bfloat16
    n_groups = len(POOL_WINDOWS)

    def resident(shape):
        return pl.BlockSpec(shape, lambda i: (0,) * len(shape), pipeline_mode=pl.Buffered(1))

    hbm = pl.BlockSpec(memory_space=pl.ANY)
    tile = pl.BlockSpec((TM, D_MODEL), lambda i: (i, 0))
    out = pl.pallas_call(
        functools.partial(_block_kernel, seq // TM),
        grid=(n_tok // TM,),
        in_specs=[
            tile,
            resident((1, D_MODEL)),
            hbm,
            hbm,
            resident((1, POOL_WIDTH)),
            resident((CONV_K, CONV_WIDTH)),
            hbm,
            resident((1, D_MODEL)),
            hbm,
            hbm,
            hbm,
            resident((1, D_MODEL)),
        ],
        out_specs=tile,
        out_shape=jax.ShapeDtypeStruct((n_tok, D_MODEL), x.dtype),
        scratch_shapes=[
            pltpu.VMEM((D_MODEL, IN_COLS), bf16),
            pltpu.VMEM((MIX_WIDTH, D_MODEL), bf16),
            pltpu.VMEM((D_MODEL, D_FF), bf16),
            pltpu.VMEM((D_MODEL, D_FF), bf16),
            pltpu.VMEM((D_FF, D_MODEL), bf16),
            pltpu.VMEM((STAGE_SLOTS, STAGE_ROWS, STAGE_COLS), jnp.float32),
            pltpu.VMEM((n_groups, POOL_GROUP_DIM, POOL_GROUP_DIM), jnp.float32),
            pltpu.SemaphoreType.DMA((STAGE_SLOTS,)),
            pltpu.SemaphoreType.DMA(()),
            pltpu.VMEM((POOL_HALO + TM, POOL_WIDTH), jnp.float32),
            pltpu.VMEM((CONV_HALO + TM, CONV_WIDTH), jnp.float32),
            pltpu.VMEM((TM, CONV_WIDTH), jnp.float32),
            pltpu.VMEM((TM, MIX_WIDTH), bf16),
            pltpu.VMEM((TM, D_MODEL), bf16),
        ],
        compiler_params=pltpu.CompilerParams(
            dimension_semantics=("arbitrary",),
            vmem_limit_bytes=VMEM_LIMIT_BYTES,
        ),
        name="hybrid_block",
    )(
        x.reshape(n_tok, D_MODEL),
        norm1_g.reshape(1, D_MODEL),
        w_in,
        pool_w,
        pool_scale.reshape(1, POOL_WIDTH),
        conv_w,
        w_out,
        norm2_g.reshape(1, D_MODEL),
        w_gate,
        w_up,
        w_down,
        normf_g.reshape(1, D_MODEL),
    )
    return out.reshape(batch, seq, D_MODEL)
```

```python
import functools

import jax
import jax.numpy as jnp
from jax import lax
from jax.experimental import pallas as pl
from jax.experimental.pallas import tpu as pltpu

D_MODEL = 1024
POOL_WIDTH = 512
CONV_WIDTH = 512
MIX_WIDTH = POOL_WIDTH + CONV_WIDTH
POOL_WINDOWS = (2, 4, 8, 16)
POOL_GROUP_DIM = 128
CONV_K = 3
IN_COLS = POOL_WIDTH + 3 * CONV_WIDTH
D_FF = 2816
RMS_EPS = 1e-6

SUB = 256
SUBTILES = 4
TM = SUB * SUBTILES
POOL_HALO = 16
assert all(w & (w - 1) == 0 and w <= POOL_HALO for w in POOL_WINDOWS)
CONV_HALO = 8
FF_CHUNK = 256
FINAL_ROWS = 256
STAGE_ROWS = 1024
STAGE_COLS = 512
STAGE_SLOTS = 4
VMEM_LIMIT_BYTES = 60 * 1024 * 1024


def _rmsnorm(x, g):
    ms = jnp.mean(x * x, axis=-1, keepdims=True)
    return (x * lax.rsqrt(ms + RMS_EPS)) * g


def _weight_chunks(w_in, w_out, w_gate, w_up, w_down, wb_in, wb_out, wb_gate, wb_up, wb_down):
    assert STAGE_COLS == POOL_WIDTH and STAGE_ROWS >= D_MODEL
    chunks = []
    for src, dst in ((w_in, wb_in), (w_out, wb_out), (w_gate, wb_gate), (w_up, wb_up), (w_down, wb_down)):
        n_rows, n_cols = src.shape
        for r0 in range(0, n_rows, STAGE_ROWS):
            rows = min(STAGE_ROWS, n_rows - r0)
            for c0 in range(0, n_cols, STAGE_COLS):
                cols = min(STAGE_COLS, n_cols - c0)
                idx = (pl.ds(r0, rows), pl.ds(c0, cols))
                chunks.append((src.at[idx], dst.at[idx], rows, cols, src is w_in and c0 == 0))
    return chunks


def _load_weights(w_in, pool_w, pool_scale_ref, w_out, w_gate, w_up, w_down,
                  wb_in, wb_out, wb_gate, wb_up, wb_down,
                  stage, pool_stage, sem, pool_sem):
    pool_copy = pltpu.make_async_copy(pool_w, pool_stage, pool_sem)
    pool_copy.start()
    chunks = _weight_chunks(w_in, w_out, w_gate, w_up, w_down, wb_in, wb_out, wb_gate, wb_up, wb_down)

    def copy(k):
        src, _, rows, cols, _ = chunks[k]
        slot = k % STAGE_SLOTS
        return pltpu.make_async_copy(src, stage.at[slot, pl.ds(0, rows), pl.ds(0, cols)], sem.at[slot])

    for k in range(min(STAGE_SLOTS - 1, len(chunks))):
        copy(k).start()
    for k, (_, dst, rows, cols, is_pool_section) in enumerate(chunks):
        if k + STAGE_SLOTS - 1 < len(chunks):
            copy(k + STAGE_SLOTS - 1).start()
        copy(k).wait()
        staged = stage.at[k % STAGE_SLOTS]
        if is_pool_section:
            pool_copy.wait()
            for g in range(len(POOL_WINDOWS)):
                gcols = slice(g * POOL_GROUP_DIM, (g + 1) * POOL_GROUP_DIM)
                group_map = pool_stage[g] * pool_scale_ref[:, gcols]
                dst[:, gcols] = jnp.dot(staged[0:rows, gcols], group_map,
                                        preferred_element_type=jnp.float32,
                                        precision=lax.Precision.HIGHEST).astype(jnp.bfloat16)
        else:
            dst[...] = staged[0:rows, 0:cols].astype(jnp.bfloat16)


def _block_kernel(tiles_per_seq,
                  x_ref, g1_ref, w_in_hbm, pool_w_hbm, pool_scale_ref, conv_w_ref,
                  w_out_hbm, g2_ref, w_gate_hbm, w_up_hbm, w_down_hbm, gf_ref,
                  o_ref,
                  w_in_ref, w_out_ref, w_gate_ref, w_up_ref, w_down_ref,
                  stage, pool_stage, sem, pool_sem,
                  vbuf, ubuf, gbbuf, ymix, hn2buf):
    i = pl.program_id(0)
    f32 = jnp.float32
    bf16 = jnp.bfloat16

    @pl.when(i == 0)
    def _():
        _load_weights(w_in_hbm, pool_w_hbm, pool_scale_ref, w_out_hbm, w_gate_hbm, w_up_hbm, w_down_hbm,
                      w_in_ref, w_out_ref, w_gate_ref, w_up_ref, w_down_ref,
                      stage, pool_stage, sem, pool_sem)

    @pl.when(i % tiles_per_seq == 0)
    def _():
        vbuf[0:POOL_HALO, :] = jnp.zeros((POOL_HALO, POOL_WIDTH), f32)
        ubuf[0:CONV_HALO, :] = jnp.zeros((CONV_HALO, CONV_WIDTH), f32)

    def in_proj():
        hn = jnp.concatenate(
            [_rmsnorm(x_ref[pl.ds(s * SUB, SUB), :], g1_ref[...]).astype(bf16) for s in range(SUBTILES)], axis=0)

        def section(k):
            cols = slice(k * POOL_WIDTH, (k + 1) * POOL_WIDTH)
            return jnp.dot(hn, w_in_ref[:, cols], preferred_element_type=f32)

        vbuf[pl.ds(POOL_HALO, TM), :] = section(0)
        urows = pl.ds(CONV_HALO, TM)
        ubuf[urows, :] = section(2)
        ubuf[urows, :] = ubuf[urows, :] * section(3)
        gbbuf[...] = section(1)

    def pool_mixer(r0, n):
        rows = pl.ds(r0, n)
        pos = (i % tiles_per_seq) * TM + r0 + lax.broadcasted_iota(jnp.int32, (n, 1), 0)
        n_seen = (pos + 1).astype(f32)
        for g, w in enumerate(POOL_WINDOWS):
            cols = slice(g * POOL_GROUP_DIM, (g + 1) * POOL_GROUP_DIM)
            ext = vbuf[pl.ds(r0, POOL_HALO + n), cols]
            tot = ext
            for k in range(w.bit_length() - 1):
                tot = tot + pltpu.roll(tot, 1 << k, 0)
            inv_cnt = 1.0 / jnp.minimum(n_seen, float(w))
            ymix[rows, cols] = (tot[POOL_HALO:] * inv_cnt - ext[POOL_HALO:]).astype(bf16)

    def conv_mixer(r0, n):
        rows = pl.ds(r0, n)
        u0 = CONV_HALO + r0
        conv = ubuf[pl.ds(u0, n), :] * conv_w_ref[CONV_K - 1:CONV_K, :]
        for k in range(CONV_K - 1):
            lag = CONV_K - 1 - k
            conv = conv + ubuf[pl.ds(u0 - lag, n), :] * conv_w_ref[k:k + 1, :]
        ymix[rows, POOL_WIDTH:] = (gbbuf[rows, :] * conv).astype(bf16)

    def out_proj():
        o_ref[...] = x_ref[...] + jnp.dot(ymix[:, :POOL_WIDTH], w_out_ref[:POOL_WIDTH, :],
                                          preferred_element_type=f32)
        for s in range(SUBTILES):
            rows = pl.ds(s * SUB, SUB)
            conv_mixer(s * SUB, SUB)
            o_ref[rows, :] += jnp.dot(ymix[rows, POOL_WIDTH:], w_out_ref[POOL_WIDTH:, :],
                                      preferred_element_type=f32)

    def ffn():
        for s in range(SUBTILES):
            rows = pl.ds(s * SUB, SUB)
            hn2buf[rows, :] = _rmsnorm(o_ref[rows, :], g2_ref[...]).astype(bf16)
        n_chunks = D_FF // FF_CHUNK
        for c in range(n_chunks):
            cols = slice(c * FF_CHUNK, (c + 1) * FF_CHUNK)
            gate = jnp.dot(hn2buf[...], w_gate_ref[:, cols], preferred_element_type=f32)
            up = jnp.dot(hn2buf[...], w_up_ref[:, cols], preferred_element_type=f32)
            a = (gate * jax.nn.sigmoid(gate) * up).astype(bf16)
            if c < n_chunks - 1:
                o_ref[...] += jnp.dot(a, w_down_ref[cols, :], preferred_element_type=f32)
            else:
                for r0 in range(0, TM, FINAL_ROWS):
                    rows = pl.ds(r0, FINAL_ROWS)
                    y = o_ref[rows, :] + jnp.dot(a[r0:r0 + FINAL_ROWS], w_down_ref[cols, :],
                                                 preferred_element_type=f32)
                    o_ref[rows, :] = _rmsnorm(y, gf_ref[...])

    in_proj()
    pool_mixer(0, TM)
    out_proj()
    vbuf[0:POOL_HALO, :] = vbuf[TM:TM + POOL_HALO, :]
    ubuf[0:CONV_HALO, :] = ubuf[TM:TM + CONV_HALO, :]
    ffn()


def kernel(x, norm1_g, w_in, pool_w, pool_scale, conv_w, w_out, norm2_g, w_gate, w_up, w_down, normf_g):
    batch, seq, d = x.shape
    assert d == D_MODEL and seq % TM == 0
    n_tok = batch * seq
    bf16 = jnp.bfloat16
    n_groups = len(POOL_WINDOWS)

    def resident(shape):
        return pl.BlockSpec(shape, lambda i: (0,) * len(shape), pipeline_mode=pl.Buffered(1))

    hbm = pl.BlockSpec(memory_space=pl.ANY)
    tile = pl.BlockSpec((TM, D_MODEL), lambda i: (i, 0))
    out = pl.pallas_call(
        functools.partial(_block_kernel, seq // TM),
        grid=(n_tok // TM,),
        in_specs=[
            tile,
            resident((1, D_MODEL)),
            hbm,
            hbm,
            resident((1, POOL_WIDTH)),
            resident((CONV_K, CONV_WIDTH)),
            hbm,
            resident((1, D_MODEL)),
            hbm,
            hbm,
            hbm,
            resident((1, D_MODEL)),
        ],
        out_specs=tile,
        out_shape=jax.ShapeDtypeStruct((n_tok, D_MODEL), x.dtype),
        scratch_shapes=[
            pltpu.VMEM((D_MODEL, IN_COLS), bf16),
            pltpu.VMEM((MIX_WIDTH, D_MODEL), bf16),
            pltpu.VMEM((D_MODEL, D_FF), bf16),
            pltpu.VMEM((D_MODEL, D_FF), bf16),
            pltpu.VMEM((D_FF, D_MODEL), bf16),
            pltpu.VMEM((STAGE_SLOTS, STAGE_ROWS, STAGE_COLS), jnp.float32),
            pltpu.VMEM((n_groups, POOL_GROUP_DIM, POOL_GROUP_DIM), jnp.float32),
            pltpu.SemaphoreType.DMA((STAGE_SLOTS,)),
            pltpu.SemaphoreType.DMA(()),
            pltpu.VMEM((POOL_HALO + TM, POOL_WIDTH), jnp.float32),
            pltpu.VMEM((CONV_HALO + TM, CONV_WIDTH), jnp.float32),
            pltpu.VMEM((TM, CONV_WIDTH), jnp.float32),
            pltpu.VMEM((TM, MIX_WIDTH), bf16),
            pltpu.VMEM((TM, D_MODEL), bf16),
        ],
        compiler_params=pltpu.CompilerParams(
            dimension_semantics=("arbitrary",),
            vmem_limit_bytes=VMEM_LIMIT_BYTES,
        ),
        name="hybrid_block",
    )(
        x.reshape(n_tok, D_MODEL),
        norm1_g.reshape(1, D_MODEL),
        w_in,
        pool_w,
        pool_scale.reshape(1, POOL_WIDTH),
        conv_w,
        w_out,
        norm2_g.reshape(1, D_MODEL),
        w_gate,
        w_up,
        w_down,
        normf_g.reshape(1, D_MODEL),
    )
    return out.reshape(batch, seq, D_MODEL)
```

```python
import functools

import jax
import jax.numpy as jnp
from jax import lax
from jax.experimental import pallas as pl
from jax.experimental.pallas import tpu as pltpu

D_MODEL = 1024
POOL_WIDTH = 512
CONV_WIDTH = 512
MIX_WIDTH = POOL_WIDTH + CONV_WIDTH
POOL_WINDOWS = (2, 4, 8, 16)
POOL_GROUP_DIM = 128
CONV_K = 3
IN_COLS = POOL_WIDTH + 3 * CONV_WIDTH
D_FF = 2816
RMS_EPS = 1e-6

SUB = 256
SUBTILES = 4
TM = SUB * SUBTILES
POOL_HALO = 16
assert all(w & (w - 1) == 0 and w <= POOL_HALO for w in POOL_WINDOWS)
CONV_HALO = 8
FF_CHUNK = 256
FF_GROUP = 2
FINAL_ROWS = 256
STAGE_ROWS = 1024
STAGE_COLS = 512
STAGE_SLOTS = 4
VMEM_LIMIT_BYTES = 60 * 1024 * 1024


def _rmsnorm(x, g):
    ms = jnp.mean(x * x, axis=-1, keepdims=True)
    return (x * lax.rsqrt(ms + RMS_EPS)) * g


def _weight_chunks(w_in, w_out, w_gate, w_up, w_down, wb_in, wb_out, wb_gate, wb_up, wb_down):
    assert STAGE_COLS == POOL_WIDTH and STAGE_ROWS >= D_MODEL
    chunks = []
    for src, dst in ((w_in, wb_in), (w_out, wb_out), (w_gate, wb_gate), (w_up, wb_up), (w_down, wb_down)):
        n_rows, n_cols = src.shape
        for r0 in range(0, n_rows, STAGE_ROWS):
            rows = min(STAGE_ROWS, n_rows - r0)
            for c0 in range(0, n_cols, STAGE_COLS):
                cols = min(STAGE_COLS, n_cols - c0)
                idx = (pl.ds(r0, rows), pl.ds(c0, cols))
                chunks.append((src.at[idx], dst.at[idx], rows, cols, src is w_in and c0 == 0))
    return chunks


def _load_weights(w_in, pool_w, pool_scale_ref, w_out, w_gate, w_up, w_down,
                  wb_in, wb_out, wb_gate, wb_up, wb_down,
                  stage, pool_stage, sem, pool_sem):
    pool_copy = pltpu.make_async_copy(pool_w, pool_stage, pool_sem)
    pool_copy.start()
    chunks = _weight_chunks(w_in, w_out, w_gate, w_up, w_down, wb_in, wb_out, wb_gate, wb_up, wb_down)

    def copy(k):
        src, _, rows, cols, _ = chunks[k]
        slot = k % STAGE_SLOTS
        return pltpu.make_async_copy(src, stage.at[slot, pl.ds(0, rows), pl.ds(0, cols)], sem.at[slot])

    for k in range(min(STAGE_SLOTS - 1, len(chunks))):
        copy(k).start()
    for k, (_, dst, rows, cols, is_pool_section) in enumerate(chunks):
        if k + STAGE_SLOTS - 1 < len(chunks):
            copy(k + STAGE_SLOTS - 1).start()
        copy(k).wait()
        staged = stage.at[k % STAGE_SLOTS]
        if is_pool_section:
            pool_copy.wait()
            for g in range(len(POOL_WINDOWS)):
                gcols = slice(g * POOL_GROUP_DIM, (g + 1) * POOL_GROUP_DIM)
                group_map = pool_stage[g] * pool_scale_ref[:, gcols]
                dst[:, gcols] = jnp.dot(staged[0:rows, gcols], group_map,
                                        preferred_element_type=jnp.float32,
                                        precision=lax.Precision.HIGHEST).astype(jnp.bfloat16)
        else:
            dst[...] = staged[0:rows, 0:cols].astype(jnp.bfloat16)


def _block_kernel(tiles_per_seq,
                  x_ref, g1_ref, w_in_hbm, pool_w_hbm, pool_scale_ref, conv_w_ref,
                  w_out_hbm, g2_ref, w_gate_hbm, w_up_hbm, w_down_hbm, gf_ref,
                  o_ref,
                  w_in_ref, w_out_ref, w_gate_ref, w_up_ref, w_down_ref,
                  stage, pool_stage, sem, pool_sem,
                  vbuf, ubuf, gbbuf, ymix, hn2buf):
    i = pl.program_id(0)
    f32 = jnp.float32
    bf16 = jnp.bfloat16

    @pl.when(i == 0)
    def _():
        _load_weights(w_in_hbm, pool_w_hbm, pool_scale_ref, w_out_hbm, w_gate_hbm, w_up_hbm, w_down_hbm,
                      w_in_ref, w_out_ref, w_gate_ref, w_up_ref, w_down_ref,
                      stage, pool_stage, sem, pool_sem)

    @pl.when(i % tiles_per_seq == 0)
    def _():
        vbuf[0:POOL_HALO, :] = jnp.zeros((POOL_HALO, POOL_WIDTH), f32)
        ubuf[0:CONV_HALO, :] = jnp.zeros((CONV_HALO, CONV_WIDTH), f32)

    def in_proj():
        hn = jnp.concatenate(
            [_rmsnorm(x_ref[pl.ds(s * SUB, SUB), :], g1_ref[...]).astype(bf16) for s in range(SUBTILES)], axis=0)

        def section(k):
            cols = slice(k * POOL_WIDTH, (k + 1) * POOL_WIDTH)
            return jnp.dot(hn, w_in_ref[:, cols], preferred_element_type=f32)

        vbuf[pl.ds(POOL_HALO, TM), :] = section(0)
        urows = pl.ds(CONV_HALO, TM)
        ubuf[urows, :] = section(2)
        ubuf[urows, :] = ubuf[urows, :] * section(3)
        gbbuf[...] = section(1)

    def pool_mixer(s):
        rows = pl.ds(s * SUB, SUB)
        pos = (i % tiles_per_seq) * TM + s * SUB + lax.broadcasted_iota(jnp.int32, (SUB, 1), 0)
        n_seen = (pos + 1).astype(f32)
        for g, w in enumerate(POOL_WINDOWS):
            cols = slice(g * POOL_GROUP_DIM, (g + 1) * POOL_GROUP_DIM)
            ext = vbuf[pl.ds(s * SUB, POOL_HALO + SUB), cols]
            tot = ext
            for k in range(w.bit_length() - 1):
                tot = tot + pltpu.roll(tot, 1 << k, 0)
            inv_cnt = 1.0 / jnp.minimum(n_seen, float(w))
            ymix[rows, cols] = (tot[POOL_HALO:] * inv_cnt - ext[POOL_HALO:]).astype(bf16)

    def conv_mixer(s):
        rows = pl.ds(s * SUB, SUB)
        u0 = CONV_HALO + s * SUB
        conv = ubuf[pl.ds(u0, SUB), :] * conv_w_ref[CONV_K - 1:CONV_K, :]
        for k in range(CONV_K - 1):
            lag = CONV_K - 1 - k
            conv = conv + ubuf[pl.ds(u0 - lag, SUB), :] * conv_w_ref[k:k + 1, :]
        ymix[rows, POOL_WIDTH:] = (gbbuf[rows, :] * conv).astype(bf16)

    def out_proj(s):
        rows = pl.ds(s * SUB, SUB)
        o_ref[rows, :] = x_ref[rows, :] + jnp.dot(ymix[rows, :], w_out_ref[...],
                                                  preferred_element_type=f32)

    def ffn():
        for s in range(SUBTILES):
            rows = pl.ds(s * SUB, SUB)
            hn2buf[rows, :] = _rmsnorm(o_ref[rows, :], g2_ref[...]).astype(bf16)
        def activation(c):
            cols = slice(c * FF_CHUNK, (c + 1) * FF_CHUNK)
            gate = jnp.dot(hn2buf[...], w_gate_ref[:, cols], preferred_element_type=f32)
            up = jnp.dot(hn2buf[...], w_up_ref[:, cols], preferred_element_type=f32)
            return (gate * jax.nn.sigmoid(gate) * up).astype(bf16)

        n_chunks = D_FF // FF_CHUNK
        for c0 in range(0, n_chunks, FF_GROUP):
            c1 = min(c0 + FF_GROUP, n_chunks)
            a = jnp.concatenate([activation(c) for c in range(c0, c1)], axis=1)
            w_down_rows = w_down_ref[c0 * FF_CHUNK:c1 * FF_CHUNK, :]
            if c1 < n_chunks:
                o_ref[...] += jnp.dot(a, w_down_rows, preferred_element_type=f32)
            else:
                for r0 in range(0, TM, FINAL_ROWS):
                    rows = pl.ds(r0, FINAL_ROWS)
                    y = o_ref[rows, :] + jnp.dot(a[r0:r0 + FINAL_ROWS], w_down_rows, preferred_element_type=f32)
                    o_ref[rows, :] = _rmsnorm(y, gf_ref[...])

    in_proj()
    for s in range(SUBTILES):
        pool_mixer(s)
        conv_mixer(s)
        out_proj(s)
    vbuf[0:POOL_HALO, :] = vbuf[TM:TM + POOL_HALO, :]
    ubuf[0:CONV_HALO, :] = ubuf[TM:TM + CONV_HALO, :]
    ffn()


def kernel(x, norm1_g, w_in, pool_w, pool_scale, conv_w, w_out, norm2_g, w_gate, w_up, w_down, normf_g):
    batch, seq, d = x.shape
    assert d == D_MODEL and seq % TM == 0
    n_tok = batch * seq
    bf16 = jnp.bfloat16
    n_groups = len(POOL_WINDOWS)

    def resident(shape):
        return pl.BlockSpec(shape, lambda i: (0,) * len(shape), pipeline_mode=pl.Buffered(1))

    hbm = pl.BlockSpec(memory_space=pl.ANY)
    tile = pl.BlockSpec((TM, D_MODEL), lambda i: (i, 0))
    out = pl.pallas_call(
        functools.partial(_block_kernel, seq // TM),
        grid=(n_tok // TM,),
        in_specs=[
            tile,
            resident((1, D_MODEL)),
            hbm,
            hbm,
            resident((1, POOL_WIDTH)),
            resident((CONV_K, CONV_WIDTH)),
            hbm,
            resident((1, D_MODEL)),
            hbm,
            hbm,
            hbm,
            resident((1, D_MODEL)),
        ],
        out_specs=tile,
        out_shape=jax.ShapeDtypeStruct((n_tok, D_MODEL), x.dtype),
        scratch_shapes=[
            pltpu.VMEM((D_MODEL, IN_COLS), bf16),
            pltpu.VMEM((MIX_WIDTH, D_MODEL), bf16),
            pltpu.VMEM((D_MODEL, D_FF), bf16),
            pltpu.VMEM((D_MODEL, D_FF), bf16),
            pltpu.VMEM((D_FF, D_MODEL), bf16),
            pltpu.VMEM((STAGE_SLOTS, STAGE_ROWS, STAGE_COLS), jnp.float32),
            pltpu.VMEM((n_groups, POOL_GROUP_DIM, POOL_GROUP_DIM), jnp.float32),
            pltpu.SemaphoreType.DMA((STAGE_SLOTS,)),
            pltpu.SemaphoreType.DMA(()),
            pltpu.VMEM((POOL_HALO + TM, POOL_WIDTH), jnp.float32),
            pltpu.VMEM((CONV_HALO + TM, CONV_WIDTH), jnp.float32),
            pltpu.VMEM((TM, CONV_WIDTH), jnp.float32),
            pltpu.VMEM((TM, MIX_WIDTH), bf16),
            pltpu.VMEM((TM, D_MODEL), bf16),
        ],
        compiler_params=pltpu.CompilerParams(
            dimension_semantics=("arbitrary",),
            vmem_limit_bytes=VMEM_LIMIT_BYTES,
        ),
        name="hybrid_block",
    )(
        x.reshape(n_tok, D_MODEL),
        norm1_g.reshape(1, D_MODEL),
        w_in,
        pool_w,
        pool_scale.reshape(1, POOL_WIDTH),
        conv_w,
        w_out,
        norm2_g.reshape(1, D_MODEL),
        w_gate,
        w_up,
        w_down,
        normf_g.reshape(1, D_MODEL),
    )
    return out.reshape(batch, seq, D_MODEL)
```

```python
import functools

import jax
import jax.numpy as jnp
from jax import lax
from jax.experimental import pallas as pl
from jax.experimental.pallas import tpu as pltpu

D_MODEL = 1024
POOL_WIDTH = 512
CONV_WIDTH = 512
MIX_WIDTH = POOL_WIDTH + CONV_WIDTH
POOL_WINDOWS = (2, 4, 8, 16)
POOL_GROUP_DIM = 128
CONV_K = 3
IN_COLS = POOL_WIDTH + 3 * CONV_WIDTH
D_FF = 2816
RMS_EPS = 1e-6

SUB = 256
SUBTILES = 4
TM = SUB * SUBTILES
POOL_HALO = 16
assert all(w & (w - 1) == 0 and w <= POOL_HALO for w in POOL_WINDOWS)
CONV_HALO = 8
FF_CHUNK = 256
FF_GROUP = 4
FINAL_ROWS = 256
STAGE_ROWS = 1024
STAGE_COLS = 512
STAGE_SLOTS = 4
VMEM_LIMIT_BYTES = 60 * 1024 * 1024


def _rmsnorm(x, g):
    ms = jnp.mean(x * x, axis=-1, keepdims=True)
    return (x * lax.rsqrt(ms + RMS_EPS)) * g


def _weight_chunks(w_in, w_out, w_gate, w_up, w_down, wb_in, wb_out, wb_gate, wb_up, wb_down):
    assert STAGE_COLS == POOL_WIDTH and STAGE_ROWS >= D_MODEL
    chunks = []
    for src, dst in ((w_in, wb_in), (w_out, wb_out), (w_gate, wb_gate), (w_up, wb_up), (w_down, wb_down)):
        n_rows, n_cols = src.shape
        for r0 in range(0, n_rows, STAGE_ROWS):
            rows = min(STAGE_ROWS, n_rows - r0)
            for c0 in range(0, n_cols, STAGE_COLS):
                cols = min(STAGE_COLS, n_cols - c0)
                idx = (pl.ds(r0, rows), pl.ds(c0, cols))
                chunks.append((src.at[idx], dst.at[idx], rows, cols, src is w_in and c0 == 0))
    return chunks


def _load_weights(w_in, pool_w, pool_scale_ref, w_out, w_gate, w_up, w_down,
                  wb_in, wb_out, wb_gate, wb_up, wb_down,
                  stage, pool_stage, sem, pool_sem):
    pool_copy = pltpu.make_async_copy(pool_w, pool_stage, pool_sem)
    pool_copy.start()
    chunks = _weight_chunks(w_in, w_out, w_gate, w_up, w_down, wb_in, wb_out, wb_gate, wb_up, wb_down)

    def copy(k):
        src, _, rows, cols, _ = chunks[k]
        slot = k % STAGE_SLOTS
        return pltpu.make_async_copy(src, stage.at[slot, pl.ds(0, rows), pl.ds(0, cols)], sem.at[slot])

    for k in range(min(STAGE_SLOTS - 1, len(chunks))):
        copy(k).start()
    for k, (_, dst, rows, cols, is_pool_section) in enumerate(chunks):
        if k + STAGE_SLOTS - 1 < len(chunks):
            copy(k + STAGE_SLOTS - 1).start()
        copy(k).wait()
        staged = stage.at[k % STAGE_SLOTS]
        if is_pool_section:
            pool_copy.wait()
            for g in range(len(POOL_WINDOWS)):
                gcols = slice(g * POOL_GROUP_DIM, (g + 1) * POOL_GROUP_DIM)
                group_map = pool_stage[g] * pool_scale_ref[:, gcols]
                dst[:, gcols] = jnp.dot(staged[0:rows, gcols], group_map,
                                        preferred_element_type=jnp.float32,
                                        precision=lax.Precision.HIGHEST).astype(jnp.bfloat16)
        else:
            dst[...] = staged[0:rows, 0:cols].astype(jnp.bfloat16)


def _block_kernel(tiles_per_seq,
                  x_ref, g1_ref, w_in_hbm, pool_w_hbm, pool_scale_ref, conv_w_ref,
                  w_out_hbm, g2_ref, w_gate_hbm, w_up_hbm, w_down_hbm, gf_ref,
                  o_ref,
                  w_in_ref, w_out_ref, w_gate_ref, w_up_ref, w_down_ref,
                  stage, pool_stage, sem, pool_sem,
                  vbuf, ubuf, gbbuf, ymix, hn2buf):
    i = pl.program_id(0)
    f32 = jnp.float32
    bf16 = jnp.bfloat16

    @pl.when(i == 0)
    def _():
        _load_weights(w_in_hbm, pool_w_hbm, pool_scale_ref, w_out_hbm, w_gate_hbm, w_up_hbm, w_down_hbm,
                      w_in_ref, w_out_ref, w_gate_ref, w_up_ref, w_down_ref,
                      stage, pool_stage, sem, pool_sem)

    @pl.when(i % tiles_per_seq == 0)
    def _():
        vbuf[0:POOL_HALO, :] = jnp.zeros((POOL_HALO, POOL_WIDTH), f32)
        ubuf[0:CONV_HALO, :] = jnp.zeros((CONV_HALO, CONV_WIDTH), f32)

    def in_proj():
        hn = jnp.concatenate(
            [_rmsnorm(x_ref[pl.ds(s * SUB, SUB), :], g1_ref[...]).astype(bf16) for s in range(SUBTILES)], axis=0)

        def section(k):
            cols = slice(k * POOL_WIDTH, (k + 1) * POOL_WIDTH)
            return jnp.dot(hn, w_in_ref[:, cols], preferred_element_type=f32)

        vbuf[pl.ds(POOL_HALO, TM), :] = section(0)
        urows = pl.ds(CONV_HALO, TM)
        ubuf[urows, :] = section(2)
        ubuf[urows, :] = ubuf[urows, :] * section(3)
        gbbuf[...] = section(1)

    def pool_mixer(s):
        rows = pl.ds(s * SUB, SUB)
        pos = (i % tiles_per_seq) * TM + s * SUB + lax.broadcasted_iota(jnp.int32, (SUB, 1), 0)
        n_seen = (pos + 1).astype(f32)
        for g, w in enumerate(POOL_WINDOWS):
            cols = slice(g * POOL_GROUP_DIM, (g + 1) * POOL_GROUP_DIM)
            ext = vbuf[pl.ds(s * SUB, POOL_HALO + SUB), cols]
            tot = ext
            for k in range(w.bit_length() - 1):
                tot = tot + pltpu.roll(tot, 1 << k, 0)
            inv_cnt = 1.0 / jnp.minimum(n_seen, float(w))
            ymix[rows, cols] = (tot[POOL_HALO:] * inv_cnt - ext[POOL_HALO:]).astype(bf16)

    def conv_mixer(s):
        rows = pl.ds(s * SUB, SUB)
        u0 = CONV_HALO + s * SUB
        conv = ubuf[pl.ds(u0, SUB), :] * conv_w_ref[CONV_K - 1:CONV_K, :]
        for k in range(CONV_K - 1):
            lag = CONV_K - 1 - k
            conv = conv + ubuf[pl.ds(u0 - lag, SUB), :] * conv_w_ref[k:k + 1, :]
        ymix[rows, POOL_WIDTH:] = (gbbuf[rows, :] * conv).astype(bf16)

    def out_proj(s):
        rows = pl.ds(s * SUB, SUB)
        o_ref[rows, :] = x_ref[rows, :] + jnp.dot(ymix[rows, :], w_out_ref[...],
                                                  preferred_element_type=f32)

    def ffn():
        for s in range(SUBTILES):
            rows = pl.ds(s * SUB, SUB)
            hn2buf[rows, :] = _rmsnorm(o_ref[rows, :], g2_ref[...]).astype(bf16)

        def activation(c):
            cols = slice(c * FF_CHUNK, (c + 1) * FF_CHUNK)
            gate = jnp.dot(hn2buf[...], w_gate_ref[:, cols], preferred_element_type=f32)
            up = jnp.dot(hn2buf[...], w_up_ref[:, cols], preferred_element_type=f32)
            return (gate * jax.nn.sigmoid(gate) * up).astype(bf16)

        n_chunks = D_FF // FF_CHUNK
        for c0 in range(0, n_chunks, FF_GROUP):
            c1 = min(c0 + FF_GROUP, n_chunks)
            a = jnp.concatenate([activation(c) for c in range(c0, c1)], axis=1)
            w_down_rows = w_down_ref[c0 * FF_CHUNK:c1 * FF_CHUNK, :]
            if c1 < n_chunks:
                o_ref[...] += jnp.dot(a, w_down_rows, preferred_element_type=f32)
            else:
                for r0 in range(0, TM, FINAL_ROWS):
                    rows = pl.ds(r0, FINAL_ROWS)
                    y = o_ref[rows, :] + jnp.dot(a[r0:r0 + FINAL_ROWS], w_down_rows, preferred_element_type=f32)
                    o_ref[rows, :] = _rmsnorm(y, gf_ref[...])

    in_proj()
    for s in range(SUBTILES):
        pool_mixer(s)
        conv_mixer(s)
        out_proj(s)
    vbuf[0:POOL_HALO, :] = vbuf[TM:TM + POOL_HALO, :]
    ubuf[0:CONV_HALO, :] = ubuf[TM:TM + CONV_HALO, :]
    ffn()


def kernel(x, norm1_g, w_in, pool_w, pool_scale, conv_w, w_out, norm2_g, w_gate, w_up, w_down, normf_g):
    batch, seq, d = x.shape
    assert d == D_MODEL and seq % TM == 0
    n_tok = batch * seq
    bf16 = jnp.bfloat16
    n_groups = len(POOL_WINDOWS)

    def resident(shape):
        return pl.BlockSpec(shape, lambda i: (0,) * len(shape), pipeline_mode=pl.Buffered(1))

    hbm = pl.BlockSpec(memory_space=pl.ANY)
    tile = pl.BlockSpec((TM, D_MODEL), lambda i: (i, 0))
    out = pl.pallas_call(
        functools.partial(_block_kernel, seq // TM),
        grid=(n_tok // TM,),
        in_specs=[
            tile,
            resident((1, D_MODEL)),
            hbm,
            hbm,
            resident((1, POOL_WIDTH)),
            resident((CONV_K, CONV_WIDTH)),
            hbm,
            resident((1, D_MODEL)),
            hbm,
            hbm,
            hbm,
            resident((1, D_MODEL)),
        ],
        out_specs=tile,
        out_shape=jax.ShapeDtypeStruct((n_tok, D_MODEL), x.dtype),
        scratch_shapes=[
            pltpu.VMEM((D_MODEL, IN_COLS), bf16),
            pltpu.VMEM((MIX_WIDTH, D_MODEL), bf16),
            pltpu.VMEM((D_MODEL, D_FF), bf16),
            pltpu.VMEM((D_MODEL, D_FF), bf16),
            pltpu.VMEM((D_FF, D_MODEL), bf16),
            pltpu.VMEM((STAGE_SLOTS, STAGE_ROWS, STAGE_COLS), jnp.float32),
            pltpu.VMEM((n_groups, POOL_GROUP_DIM, POOL_GROUP_DIM), jnp.float32),
            pltpu.SemaphoreType.DMA((STAGE_SLOTS,)),
            pltpu.SemaphoreType.DMA(()),
            pltpu.VMEM((POOL_HALO + TM, POOL_WIDTH), jnp.float32),
            pltpu.VMEM((CONV_HALO + TM, CONV_WIDTH), jnp.float32),
            pltpu.VMEM((TM, CONV_WIDTH), jnp.float32),
            pltpu.VMEM((TM, MIX_WIDTH), bf16),
            pltpu.VMEM((TM, D_MODEL), bf16),
        ],
        compiler_params=pltpu.CompilerParams(
            dimension_semantics=("arbitrary",),
            vmem_limit_bytes=VMEM_LIMIT_BYTES,
        ),
        name="hybrid_block",
    )(
        x.reshape(n_tok, D_MODEL),
        norm1_g.reshape(1, D_MODEL),
        w_in,
        pool_w,
        pool_scale.reshape(1, POOL_WIDTH),
        conv_w,
        w_out,
        norm2_g.reshape(1, D_MODEL),
        w_gate,
        w_up,
        w_down,
        normf_g.reshape(1, D_MODEL),
    )
    return out.reshape(batch, seq, D_MODEL)
```

```python
import functools

import jax
import jax.numpy as jnp
from jax import lax
from jax.experimental import pallas as pl
from jax.experimental.pallas import tpu as pltpu

D_MODEL = 1024
POOL_WIDTH = 512
CONV_WIDTH = 512
MIX_WIDTH = POOL_WIDTH + CONV_WIDTH
POOL_WINDOWS = (2, 4, 8, 16)
POOL_GROUP_DIM = 128
CONV_K = 3
IN_COLS = POOL_WIDTH + 3 * CONV_WIDTH
D_FF = 2816
RMS_EPS = 1e-6

SUB = 256
SUBTILES = 4
TM = SUB * SUBTILES
POOL_HALO = 16
assert all(w & (w - 1) == 0 and w <= POOL_HALO for w in POOL_WINDOWS)
CONV_HALO = 8
FF_CHUNK = 256
FF_GROUP = 4
FINAL_ROWS = 256
STAGE_ROWS = 512
STAGE_COLS = 1024
STAGE_SLOTS = 4
VMEM_LIMIT_BYTES = 60 * 1024 * 1024


def _rmsnorm(x, g):
    ms = jnp.mean(x * x, axis=-1, keepdims=True)
    return (x * lax.rsqrt(ms + RMS_EPS)) * g


def _weight_chunks(w_in, w_out, w_gate, w_up, w_down, wb_in, wb_out, wb_gate, wb_up, wb_down):
    assert STAGE_COLS >= POOL_WIDTH
    chunks = []
    for src, dst in ((w_in, wb_in), (w_out, wb_out), (w_gate, wb_gate), (w_up, wb_up), (w_down, wb_down)):
        n_rows, n_cols = src.shape
        for r0 in range(0, n_rows, STAGE_ROWS):
            rows = min(STAGE_ROWS, n_rows - r0)
            for c0 in range(0, n_cols, STAGE_COLS):
                cols = min(STAGE_COLS, n_cols - c0)
                idx = (pl.ds(r0, rows), pl.ds(c0, cols))
                chunks.append((src.at[idx], dst.at[idx], rows, cols, src is w_in and c0 == 0))
    return chunks


def _load_weights(w_in, pool_w, pool_scale_ref, w_out, w_gate, w_up, w_down,
                  wb_in, wb_out, wb_gate, wb_up, wb_down,
                  stage, pool_stage, sem, pool_sem):
    pool_copy = pltpu.make_async_copy(pool_w, pool_stage, pool_sem)
    pool_copy.start()
    chunks = _weight_chunks(w_in, w_out, w_gate, w_up, w_down, wb_in, wb_out, wb_gate, wb_up, wb_down)

    def copy(k):
        src, _, rows, cols, _ = chunks[k]
        slot = k % STAGE_SLOTS
        return pltpu.make_async_copy(src, stage.at[slot, pl.ds(0, rows), pl.ds(0, cols)], sem.at[slot])

    for k in range(min(STAGE_SLOTS - 1, len(chunks))):
        copy(k).start()
    pool_w_ready = False
    for k, (_, dst, rows, cols, has_pool_section) in enumerate(chunks):
        if k + STAGE_SLOTS - 1 < len(chunks):
            copy(k + STAGE_SLOTS - 1).start()
        copy(k).wait()
        staged = stage.at[k % STAGE_SLOTS]
        plain_from = 0
        if has_pool_section:
            if not pool_w_ready:
                pool_copy.wait()
                pool_w_ready = True
            for g in range(len(POOL_WINDOWS)):
                gcols = slice(g * POOL_GROUP_DIM, (g + 1) * POOL_GROUP_DIM)
                group_map = pool_stage[g] * pool_scale_ref[:, gcols]
                dst[:, gcols] = jnp.dot(staged[0:rows, gcols], group_map,
                                        preferred_element_type=jnp.float32,
                                        precision=lax.Precision.HIGHEST).astype(jnp.bfloat16)
            plain_from = POOL_WIDTH
        if plain_from < cols:
            dst[:, plain_from:cols] = staged[0:rows, plain_from:cols].astype(jnp.bfloat16)


def _block_kernel(tiles_per_seq,
                  x_ref, g1_ref, w_in_hbm, pool_w_hbm, pool_scale_ref, conv_w_ref,
                  w_out_hbm, g2_ref, w_gate_hbm, w_up_hbm, w_down_hbm, gf_ref,
                  o_ref,
                  w_in_ref, w_out_ref, w_gate_ref, w_up_ref, w_down_ref,
                  stage, pool_stage, sem, pool_sem,
                  vbuf, ubuf, gbbuf, ymix, hn2buf):
    i = pl.program_id(0)
    f32 = jnp.float32
    bf16 = jnp.bfloat16

    @pl.when(i == 0)
    def _():
        _load_weights(w_in_hbm, pool_w_hbm, pool_scale_ref, w_out_hbm, w_gate_hbm, w_up_hbm, w_down_hbm,
                      w_in_ref, w_out_ref, w_gate_ref, w_up_ref, w_down_ref,
                      stage, pool_stage, sem, pool_sem)

    @pl.when(i % tiles_per_seq == 0)
    def _():
        vbuf[0:POOL_HALO, :] = jnp.zeros((POOL_HALO, POOL_WIDTH), f32)
        ubuf[0:CONV_HALO, :] = jnp.zeros((CONV_HALO, CONV_WIDTH), f32)

    def in_proj():
        hn = jnp.concatenate(
            [_rmsnorm(x_ref[pl.ds(s * SUB, SUB), :], g1_ref[...]).astype(bf16) for s in range(SUBTILES)], axis=0)

        def section(k):
            cols = slice(k * POOL_WIDTH, (k + 1) * POOL_WIDTH)
            return jnp.dot(hn, w_in_ref[:, cols], preferred_element_type=f32)

        vbuf[pl.ds(POOL_HALO, TM), :] = section(0)
        urows = pl.ds(CONV_HALO, TM)
        ubuf[urows, :] = section(2)
        ubuf[urows, :] = ubuf[urows, :] * section(3)
        gbbuf[...] = section(1)

    def pool_mixer(s):
        rows = pl.ds(s * SUB, SUB)
        pos = (i % tiles_per_seq) * TM + s * SUB + lax.broadcasted_iota(jnp.int32, (SUB, 1), 0)
        n_seen = (pos + 1).astype(f32)
        for g, w in enumerate(POOL_WINDOWS):
            cols = slice(g * POOL_GROUP_DIM, (g + 1) * POOL_GROUP_DIM)
            ext = vbuf[pl.ds(s * SUB, POOL_HALO + SUB), cols]
            tot = ext
            for k in range(w.bit_length() - 1):
                tot = tot + pltpu.roll(tot, 1 << k, 0)
            inv_cnt = 1.0 / jnp.minimum(n_seen, float(w))
            ymix[rows, cols] = (tot[POOL_HALO:] * inv_cnt - ext[POOL_HALO:]).astype(bf16)

    def conv_mixer(s):
        rows = pl.ds(s * SUB, SUB)
        u0 = CONV_HALO + s * SUB
        conv = ubuf[pl.ds(u0, SUB), :] * conv_w_ref[CONV_K - 1:CONV_K, :]
        for k in range(CONV_K - 1):
            lag = CONV_K - 1 - k
            conv = conv + ubuf[pl.ds(u0 - lag, SUB), :] * conv_w_ref[k:k + 1, :]
        ymix[rows, POOL_WIDTH:] = (gbbuf[rows, :] * conv).astype(bf16)

    def out_proj(s):
        rows = pl.ds(s * SUB, SUB)
        o_ref[rows, :] = x_ref[rows, :] + jnp.dot(ymix[rows, :], w_out_ref[...],
                                                  preferred_element_type=f32)

    def ffn():
        for s in range(SUBTILES):
            rows = pl.ds(s * SUB, SUB)
            hn2buf[rows, :] = _rmsnorm(o_ref[rows, :], g2_ref[...]).astype(bf16)

        def activation(c):
            cols = slice(c * FF_CHUNK, (c + 1) * FF_CHUNK)
            gate = jnp.dot(hn2buf[...], w_gate_ref[:, cols], preferred_element_type=f32)
            up = jnp.dot(hn2buf[...], w_up_ref[:, cols], preferred_element_type=f32)
            return (gate * jax.nn.sigmoid(gate) * up).astype(bf16)

        n_chunks = D_FF // FF_CHUNK
        for c0 in range(0, n_chunks, FF_GROUP):
            c1 = min(c0 + FF_GROUP, n_chunks)
            a = jnp.concatenate([activation(c) for c in range(c0, c1)], axis=1)
            w_down_rows = w_down_ref[c0 * FF_CHUNK:c1 * FF_CHUNK, :]
            if c1 < n_chunks:
                o_ref[...] += jnp.dot(a, w_down_rows, preferred_element_type=f32)
            else:
                for r0 in range(0, TM, FINAL_ROWS):
                    rows = pl.ds(r0, FINAL_ROWS)
                    y = o_ref[rows, :] + jnp.dot(a[r0:r0 + FINAL_ROWS], w_down_rows, preferred_element_type=f32)
                    o_ref[rows, :] = _rmsnorm(y, gf_ref[...])

    in_proj()
    for s in range(SUBTILES):
        pool_mixer(s)
        conv_mixer(s)
        out_proj(s)
    vbuf[0:POOL_HALO, :] = vbuf[TM:TM + POOL_HALO, :]
    ubuf[0:CONV_HALO, :] = ubuf[TM:TM + CONV_HALO, :]
    ffn()


def kernel(x, norm1_g, w_in, pool_w, pool_scale, conv_w, w_out, norm2_g, w_gate, w_up, w_down, normf_g):
    batch, seq, d = x.shape
    assert d == D_MODEL and seq % TM == 0
    n_tok = batch * seq
    bf16 = jnp.bfloat16
    n_groups = len(POOL_WINDOWS)

    def resident(shape):
        return pl.BlockSpec(shape, lambda i: (0,) * len(shape), pipeline_mode=pl.Buffered(1))

    hbm = pl.BlockSpec(memory_space=pl.ANY)
    tile = pl.BlockSpec((TM, D_MODEL), lambda i: (i, 0))
    out = pl.pallas_call(
        functools.partial(_block_kernel, seq // TM),
        grid=(n_tok // TM,),
        in_specs=[
            tile,
            resident((1, D_MODEL)),
            hbm,
            hbm,
            resident((1, POOL_WIDTH)),
            resident((CONV_K, CONV_WIDTH)),
            hbm,
            resident((1, D_MODEL)),
            hbm,
            hbm,
            hbm,
            resident((1, D_MODEL)),
        ],
        out_specs=tile,
        out_shape=jax.ShapeDtypeStruct((n_tok, D_MODEL), x.dtype),
        scratch_shapes=[
            pltpu.VMEM((D_MODEL, IN_COLS), bf16),
            pltpu.VMEM((MIX_WIDTH, D_MODEL), bf16),
            pltpu.VMEM((D_MODEL, D_FF), bf16),
            pltpu.VMEM((D_MODEL, D_FF), bf16),
            pltpu.VMEM((D_FF, D_MODEL), bf16),
            pltpu.VMEM((STAGE_SLOTS, STAGE_ROWS, STAGE_COLS), jnp.float32),
            pltpu.VMEM((n_groups, POOL_GROUP_DIM, POOL_GROUP_DIM), jnp.float32),
            pltpu.SemaphoreType.DMA((STAGE_SLOTS,)),
            pltpu.SemaphoreType.DMA(()),
            pltpu.VMEM((POOL_HALO + TM, POOL_WIDTH), jnp.float32),
            pltpu.VMEM((CONV_HALO + TM, CONV_WIDTH), jnp.float32),
            pltpu.VMEM((TM, CONV_WIDTH), jnp.float32),
            pltpu.VMEM((TM, MIX_WIDTH), bf16),
            pltpu.VMEM((TM, D_MODEL), bf16),
        ],
        compiler_params=pltpu.CompilerParams(
            dimension_semantics=("arbitrary",),
            vmem_limit_bytes=VMEM_LIMIT_BYTES,
        ),
        name="hybrid_block",
    )(
        x.reshape(n_tok, D_MODEL),
        norm1_g.reshape(1, D_MODEL),
        w_in,
        pool_w,
        pool_scale.reshape(1, POOL_WIDTH),
        conv_w,
        w_out,
        norm2_g.reshape(1, D_MODEL),
        w_gate,
        w_up,
        w_down,
        normf_g.reshape(1, D_MODEL),
    )
    return out.reshape(batch, seq, D_MODEL)
```

```python
import functools

import jax
import jax.numpy as jnp
from jax import lax
from jax.experimental import pallas as pl
from jax.experimental.pallas import tpu as pltpu

D_MODEL = 1024
POOL_WIDTH = 512
CONV_WIDTH = 512
MIX_WIDTH = POOL_WIDTH + CONV_WIDTH
POOL_WINDOWS = (2, 4, 8, 16)
POOL_GROUP_DIM = 128
CONV_K = 3
IN_COLS = POOL_WIDTH + 3 * CONV_WIDTH
D_FF = 2816
RMS_EPS = 1e-6

SUB = 256
SUBTILES = 4
TM = SUB * SUBTILES
POOL_HALO = 16
assert all(w & (w - 1) == 0 and w <= POOL_HALO for w in POOL_WINDOWS)
CONV_HALO = 8
FF_CHUNK = 256
FF_GROUP = 4
FINAL_ROWS = 256
STAGE_ROWS = 1024
STAGE_COLS = 512
STAGE_SLOTS = 4
VMEM_LIMIT_BYTES = 60 * 1024 * 1024


def _rmsnorm(x, g):
    ms = jnp.mean(x * x, axis=-1, keepdims=True)
    return (x * lax.rsqrt(ms + RMS_EPS)) * g


def _weight_chunks(w_in, w_out, w_gate, w_up, w_down, wb_in, wb_out, wb_gate, wb_up, wb_down):
    assert STAGE_COLS == POOL_WIDTH and STAGE_ROWS >= D_MODEL
    chunks = []
    for src, dst in ((w_in, wb_in), (w_out, wb_out), (w_gate, wb_gate), (w_up, wb_up), (w_down, wb_down)):
        n_rows, n_cols = src.shape
        for r0 in range(0, n_rows, STAGE_ROWS):
            rows = min(STAGE_ROWS, n_rows - r0)
            for c0 in range(0, n_cols, STAGE_COLS):
                cols = min(STAGE_COLS, n_cols - c0)
                idx = (pl.ds(r0, rows), pl.ds(c0, cols))
                chunks.append((src.at[idx], dst.at[idx], rows, cols, src is w_in and c0 == 0))
    return chunks


def _load_weights(w_in, pool_w, pool_scale_ref, w_out, w_gate, w_up, w_down,
                  wb_in, wb_out, wb_gate, wb_up, wb_down,
                  stage, pool_stage, sem, pool_sem):
    pool_copy = pltpu.make_async_copy(pool_w, pool_stage, pool_sem)
    pool_copy.start()
    chunks = _weight_chunks(w_in, w_out, w_gate, w_up, w_down, wb_in, wb_out, wb_gate, wb_up, wb_down)

    def copy(k):
        src, _, rows, cols, _ = chunks[k]
        slot = k % STAGE_SLOTS
        return pltpu.make_async_copy(src, stage.at[slot, pl.ds(0, rows), pl.ds(0, cols)], sem.at[slot])

    for k in range(min(STAGE_SLOTS - 1, len(chunks))):
        copy(k).start(priority=k % 2)
    for k, (_, dst, rows, cols, is_pool_section) in enumerate(chunks):
        if k + STAGE_SLOTS - 1 < len(chunks):
            copy(k + STAGE_SLOTS - 1).start(priority=(k + STAGE_SLOTS - 1) % 2)
        copy(k).wait()
        staged = stage.at[k % STAGE_SLOTS]
        if is_pool_section:
            pool_copy.wait()
            for g in range(len(POOL_WINDOWS)):
                gcols = slice(g * POOL_GROUP_DIM, (g + 1) * POOL_GROUP_DIM)
                group_map = pool_stage[g] * pool_scale_ref[:, gcols]
                dst[:, gcols] = jnp.dot(staged[0:rows, gcols], group_map,
                                        preferred_element_type=jnp.float32,
                                        precision=lax.Precision.HIGHEST).astype(jnp.bfloat16)
        else:
            dst[...] = staged[0:rows, 0:cols].astype(jnp.bfloat16)


def _block_kernel(tiles_per_seq,
                  x_ref, g1_ref, w_in_hbm, pool_w_hbm, pool_scale_ref, conv_w_ref,
                  w_out_hbm, g2_ref, w_gate_hbm, w_up_hbm, w_down_hbm, gf_ref,
                  o_ref,
                  w_in_ref, w_out_ref, w_gate_ref, w_up_ref, w_down_ref,
                  stage, pool_stage, sem, pool_sem,
                  vbuf, ubuf, gbbuf, ymix, hn2buf):
    i = pl.program_id(0)
    f32 = jnp.float32
    bf16 = jnp.bfloat16

    @pl.when(i == 0)
    def _():
        _load_weights(w_in_hbm, pool_w_hbm, pool_scale_ref, w_out_hbm, w_gate_hbm, w_up_hbm, w_down_hbm,
                      w_in_ref, w_out_ref, w_gate_ref, w_up_ref, w_down_ref,
                      stage, pool_stage, sem, pool_sem)

    @pl.when(i % tiles_per_seq == 0)
    def _():
        vbuf[0:POOL_HALO, :] = jnp.zeros((POOL_HALO, POOL_WIDTH), f32)
        ubuf[0:CONV_HALO, :] = jnp.zeros((CONV_HALO, CONV_WIDTH), f32)

    def in_proj():
        hn = jnp.concatenate(
            [_rmsnorm(x_ref[pl.ds(s * SUB, SUB), :], g1_ref[...]).astype(bf16) for s in range(SUBTILES)], axis=0)

        def section(k):
            cols = slice(k * POOL_WIDTH, (k + 1) * POOL_WIDTH)
            return jnp.dot(hn, w_in_ref[:, cols], preferred_element_type=f32)

        vbuf[pl.ds(POOL_HALO, TM), :] = section(0)
        urows = pl.ds(CONV_HALO, TM)
        ubuf[urows, :] = section(2)
        ubuf[urows, :] = ubuf[urows, :] * section(3)
        gbbuf[...] = section(1)

    def pool_mixer(s):
        rows = pl.ds(s * SUB, SUB)
        pos = (i % tiles_per_seq) * TM + s * SUB + lax.broadcasted_iota(jnp.int32, (SUB, 1), 0)
        n_seen = (pos + 1).astype(f32)
        for g, w in enumerate(POOL_WINDOWS):
            cols = slice(g * POOL_GROUP_DIM, (g + 1) * POOL_GROUP_DIM)
            ext = vbuf[pl.ds(s * SUB, POOL_HALO + SUB), cols]
            tot = ext
            for k in range(w.bit_length() - 1):
                tot = tot + pltpu.roll(tot, 1 << k, 0)
            inv_cnt = 1.0 / jnp.minimum(n_seen, float(w))
            ymix[rows, cols] = (tot[POOL_HALO:] * inv_cnt - ext[POOL_HALO:]).astype(bf16)

    def conv_mixer(s):
        rows = pl.ds(s * SUB, SUB)
        u0 = CONV_HALO + s * SUB
        conv = ubuf[pl.ds(u0, SUB), :] * conv_w_ref[CONV_K - 1:CONV_K, :]
        for k in range(CONV_K - 1):
            lag = CONV_K - 1 - k
            conv = conv + ubuf[pl.ds(u0 - lag, SUB), :] * conv_w_ref[k:k + 1, :]
        ymix[rows, POOL_WIDTH:] = (gbbuf[rows, :] * conv).astype(bf16)

    def out_proj(s):
        rows = pl.ds(s * SUB, SUB)
        o_ref[rows, :] = x_ref[rows, :] + jnp.dot(ymix[rows, :], w_out_ref[...],
                                                  preferred_element_type=f32)

    def ffn():
        for s in range(SUBTILES):
            rows = pl.ds(s * SUB, SUB)
            hn2buf[rows, :] = _rmsnorm(o_ref[rows, :], g2_ref[...]).astype(bf16)

        def activation(c):
            cols = slice(c * FF_CHUNK, (c + 1) * FF_CHUNK)
            gate = jnp.dot(hn2buf[...], w_gate_ref[:, cols], preferred_element_type=f32)
            up = jnp.dot(hn2buf[...], w_up_ref[:, cols], preferred_element_type=f32)
            return (gate * jax.nn.sigmoid(gate) * up).astype(bf16)

        n_chunks = D_FF // FF_CHUNK
        for c0 in range(0, n_chunks, FF_GROUP):
            c1 = min(c0 + FF_GROUP, n_chunks)
            a = jnp.concatenate([activation(c) for c in range(c0, c1)], axis=1)
            w_down_rows = w_down_ref[c0 * FF_CHUNK:c1 * FF_CHUNK, :]
            if c1 < n_chunks:
                o_ref[...] += jnp.dot(a, w_down_rows, preferred_element_type=f32)
            else:
                for r0 in range(0, TM, FINAL_ROWS):
                    rows = pl.ds(r0, FINAL_ROWS)
                    y = o_ref[rows, :] + jnp.dot(a[r0:r0 + FINAL_ROWS], w_down_rows, preferred_element_type=f32)
                    o_ref[rows, :] = _rmsnorm(y, gf_ref[...])

    in_proj()
    for s in range(SUBTILES):
        pool_mixer(s)
        conv_mixer(s)
        out_proj(s)
    vbuf[0:POOL_HALO, :] = vbuf[TM:TM + POOL_HALO, :]
    ubuf[0:CONV_HALO, :] = ubuf[TM:TM + CONV_HALO, :]
    ffn()


def kernel(x, norm1_g, w_in, pool_w, pool_scale, conv_w, w_out, norm2_g, w_gate, w_up, w_down, normf_g):
    batch, seq, d = x.shape
    assert d == D_MODEL and seq % TM == 0
    n_tok = batch * seq
    bf16 = jnp.bfloat16
    n_groups = len(POOL_WINDOWS)

    def resident(shape):
        return pl.BlockSpec(shape, lambda i: (0,) * len(shape), pipeline_mode=pl.Buffered(1))

    hbm = pl.BlockSpec(memory_space=pl.ANY)
    tile = pl.BlockSpec((TM, D_MODEL), lambda i: (i, 0))
    out = pl.pallas_call(
        functools.partial(_block_kernel, seq // TM),
        grid=(n_tok // TM,),
        in_specs=[
            tile,
            resident((1, D_MODEL)),
            hbm,
            hbm,
            resident((1, POOL_WIDTH)),
            resident((CONV_K, CONV_WIDTH)),
            hbm,
            resident((1, D_MODEL)),
            hbm,
            hbm,
            hbm,
            resident((1, D_MODEL)),
        ],
        out_specs=tile,
        out_shape=jax.ShapeDtypeStruct((n_tok, D_MODEL), x.dtype),
        scratch_shapes=[
            pltpu.VMEM((D_MODEL, IN_COLS), bf16),
            pltpu.VMEM((MIX_WIDTH, D_MODEL), bf16),
            pltpu.VMEM((D_MODEL, D_FF), bf16),
            pltpu.VMEM((D_MODEL, D_FF), bf16),
            pltpu.VMEM((D_FF, D_MODEL), bf16),
            pltpu.VMEM((STAGE_SLOTS, STAGE_ROWS, STAGE_COLS), jnp.float32),
            pltpu.VMEM((n_groups, POOL_GROUP_DIM, POOL_GROUP_DIM), jnp.float32),
            pltpu.SemaphoreType.DMA((STAGE_SLOTS,)),
            pltpu.SemaphoreType.DMA(()),
            pltpu.VMEM((POOL_HALO + TM, POOL_WIDTH), jnp.float32),
            pltpu.VMEM((CONV_HALO + TM, CONV_WIDTH), jnp.float32),
            pltpu.VMEM((TM, CONV_WIDTH), jnp.float32),
            pltpu.VMEM((TM, MIX_WIDTH), bf16),
            pltpu.VMEM((TM, D_MODEL), bf16),
        ],
        compiler_params=pltpu.CompilerParams(
            dimension_semantics=("arbitrary",),
            vmem_limit_bytes=VMEM_LIMIT_BYTES,
        ),
        name="hybrid_block",
    )(
        x.reshape(n_tok, D_MODEL),
        norm1_g.reshape(1, D_MODEL),
        w_in,
        pool_w,
        pool_scale.reshape(1, POOL_WIDTH),
        conv_w,
        w_out,
        norm2_g.reshape(1, D_MODEL),
        w_gate,
        w_up,
        w_down,
        normf_g.reshape(1, D_MODEL),
    )
    return out.reshape(batch, seq, D_MODEL)
```

```python
import functools

import jax
import jax.numpy as jnp
from jax import lax
from jax.experimental import pallas as pl
from jax.experimental.pallas import tpu as pltpu

D_MODEL = 1024
POOL_WIDTH = 512
CONV_WIDTH = 512
MIX_WIDTH = POOL_WIDTH + CONV_WIDTH
POOL_WINDOWS = (2, 4, 8, 16)
POOL_GROUP_DIM = 128
CONV_K = 3
IN_COLS = POOL_WIDTH + 3 * CONV_WIDTH
D_FF = 2816
RMS_EPS = 1e-6

SUB = 256
SUBTILES = 4
TM = SUB * SUBTILES
POOL_HALO = 16
assert all(w & (w - 1) == 0 and w <= POOL_HALO for w in POOL_WINDOWS)
CONV_HALO = 8
FF_CHUNK = 256
FF_GROUP = 4
FINAL_ROWS = 256
STAGE_ROWS = 1024
STAGE_COLS = 512
STAGE_SLOTS = 4
VMEM_LIMIT_BYTES = 62 * 1024 * 1024


def _rmsnorm(x, g):
    ms = jnp.mean(x * x, axis=-1, keepdims=True)
    return (x * lax.rsqrt(ms + RMS_EPS)) * g


def _weight_chunks(w_in, w_out, w_gate, w_up, w_down, wb_in, wb_out, wb_gate, wb_up, wb_down):
    assert STAGE_COLS == POOL_WIDTH and STAGE_ROWS >= D_MODEL
    chunks = []
    for src, dst in ((w_in, wb_in), (w_out, wb_out), (w_gate, wb_gate), (w_up, wb_up), (w_down, wb_down)):
        n_rows, n_cols = src.shape
        for r0 in range(0, n_rows, STAGE_ROWS):
            rows = min(STAGE_ROWS, n_rows - r0)
            for c0 in range(0, n_cols, STAGE_COLS):
                cols = min(STAGE_COLS, n_cols - c0)
                idx = (pl.ds(r0, rows), pl.ds(c0, cols))
                chunks.append((src.at[idx], dst.at[idx], rows, cols, src is w_in and c0 == 0))
    return chunks


def _load_weights(w_in, pool_w, pool_scale_ref, w_out, w_gate, w_up, w_down,
                  wb_in, wb_out, wb_gate, wb_up, wb_down,
                  stage, pool_stage, sem, pool_sem):
    pool_copy = pltpu.make_async_copy(pool_w, pool_stage, pool_sem)
    pool_copy.start()
    chunks = _weight_chunks(w_in, w_out, w_gate, w_up, w_down, wb_in, wb_out, wb_gate, wb_up, wb_down)

    def copy(k):
        src, _, rows, cols, _ = chunks[k]
        slot = k % STAGE_SLOTS
        return pltpu.make_async_copy(src, stage.at[slot, pl.ds(0, rows), pl.ds(0, cols)], sem.at[slot])

    for k in range(min(STAGE_SLOTS - 1, len(chunks))):
        copy(k).start()
    for k, (_, dst, rows, cols, is_pool_section) in enumerate(chunks):
        if k + STAGE_SLOTS - 1 < len(chunks):
            copy(k + STAGE_SLOTS - 1).start()
        copy(k).wait()
        staged = stage.at[k % STAGE_SLOTS]
        if is_pool_section:
            pool_copy.wait()
            for g in range(len(POOL_WINDOWS)):
                gcols = slice(g * POOL_GROUP_DIM, (g + 1) * POOL_GROUP_DIM)
                group_map = pool_stage[g] * pool_scale_ref[:, gcols]
                dst[:, gcols] = jnp.dot(staged[0:rows, gcols], group_map,
                                        preferred_element_type=jnp.float32,
                                        precision=lax.Precision.HIGHEST).astype(jnp.bfloat16)
        else:
            dst[...] = staged[0:rows, 0:cols].astype(jnp.bfloat16)


def _block_kernel(tiles_per_seq,
                  x_ref, g1_ref, w_in_hbm, pool_w_hbm, pool_scale_ref, conv_w_ref,
                  w_out_hbm, g2_ref, w_gate_hbm, w_up_hbm, w_down_hbm, gf_ref,
                  o_ref,
                  w_in_ref, w_out_ref, w_gate_ref, w_up_ref, w_down_ref,
                  stage, pool_stage, sem, pool_sem,
                  vbuf, ubuf, gbbuf, ymix, hn2buf):
    i = pl.program_id(0)
    f32 = jnp.float32
    bf16 = jnp.bfloat16

    @pl.when(i == 0)
    def _():
        _load_weights(w_in_hbm, pool_w_hbm, pool_scale_ref, w_out_hbm, w_gate_hbm, w_up_hbm, w_down_hbm,
                      w_in_ref, w_out_ref, w_gate_ref, w_up_ref, w_down_ref,
                      stage, pool_stage, sem, pool_sem)

    @pl.when(i % tiles_per_seq == 0)
    def _():
        vbuf[0:POOL_HALO, :] = jnp.zeros((POOL_HALO, POOL_WIDTH), f32)
        ubuf[0:CONV_HALO, :] = jnp.zeros((CONV_HALO, CONV_WIDTH), f32)

    def in_proj():
        hn = jnp.concatenate(
            [_rmsnorm(x_ref[pl.ds(s * SUB, SUB), :], g1_ref[...]).astype(bf16) for s in range(SUBTILES)], axis=0)

        proj = jnp.dot(hn, w_in_ref[...], preferred_element_type=f32)

        def section(k):
            return proj[:, k * POOL_WIDTH:(k + 1) * POOL_WIDTH]

        vbuf[pl.ds(POOL_HALO, TM), :] = section(0)
        gbbuf[...] = section(1)
        ubuf[pl.ds(CONV_HALO, TM), :] = section(2) * section(3)

    def pool_mixer(s):
        rows = pl.ds(s * SUB, SUB)
        pos = (i % tiles_per_seq) * TM + s * SUB + lax.broadcasted_iota(jnp.int32, (SUB, 1), 0)
        n_seen = (pos + 1).astype(f32)
        for g, w in enumerate(POOL_WINDOWS):
            cols = slice(g * POOL_GROUP_DIM, (g + 1) * POOL_GROUP_DIM)
            ext = vbuf[pl.ds(s * SUB, POOL_HALO + SUB), cols]
            tot = ext
            for k in range(w.bit_length() - 1):
                tot = tot + pltpu.roll(tot, 1 << k, 0)
            inv_cnt = 1.0 / jnp.minimum(n_seen, float(w))
            ymix[rows, cols] = (tot[POOL_HALO:] * inv_cnt - ext[POOL_HALO:]).astype(bf16)

    def conv_mixer(s):
        rows = pl.ds(s * SUB, SUB)
        u0 = CONV_HALO + s * SUB
        conv = ubuf[pl.ds(u0, SUB), :] * conv_w_ref[CONV_K - 1:CONV_K, :]
        for k in range(CONV_K - 1):
            lag = CONV_K - 1 - k
            conv = conv + ubuf[pl.ds(u0 - lag, SUB), :] * conv_w_ref[k:k + 1, :]
        ymix[rows, POOL_WIDTH:] = (gbbuf[rows, :] * conv).astype(bf16)

    def out_proj(s):
        rows = pl.ds(s * SUB, SUB)
        o_ref[rows, :] = x_ref[rows, :] + jnp.dot(ymix[rows, :], w_out_ref[...],
                                                  preferred_element_type=f32)

    def ffn():
        for s in range(SUBTILES):
            rows = pl.ds(s * SUB, SUB)
            hn2buf[rows, :] = _rmsnorm(o_ref[rows, :], g2_ref[...]).astype(bf16)

        def activation(c):
            cols = slice(c * FF_CHUNK, (c + 1) * FF_CHUNK)
            gate = jnp.dot(hn2buf[...], w_gate_ref[:, cols], preferred_element_type=f32)
            up = jnp.dot(hn2buf[...], w_up_ref[:, cols], preferred_element_type=f32)
            return (gate * jax.nn.sigmoid(gate) * up).astype(bf16)

        n_chunks = D_FF // FF_CHUNK
        for c0 in range(0, n_chunks, FF_GROUP):
            c1 = min(c0 + FF_GROUP, n_chunks)
            a = jnp.concatenate([activation(c) for c in range(c0, c1)], axis=1)
            w_down_rows = w_down_ref[c0 * FF_CHUNK:c1 * FF_CHUNK, :]
            if c1 < n_chunks:
                o_ref[...] += jnp.dot(a, w_down_rows, preferred_element_type=f32)
            else:
                for r0 in range(0, TM, FINAL_ROWS):
                    rows = pl.ds(r0, FINAL_ROWS)
                    y = o_ref[rows, :] + jnp.dot(a[r0:r0 + FINAL_ROWS], w_down_rows, preferred_element_type=f32)
                    o_ref[rows, :] = _rmsnorm(y, gf_ref[...])

    in_proj()
    for s in range(SUBTILES):
        pool_mixer(s)
        conv_mixer(s)
        out_proj(s)
    vbuf[0:POOL_HALO, :] = vbuf[TM:TM + POOL_HALO, :]
    ubuf[0:CONV_HALO, :] = ubuf[TM:TM + CONV_HALO, :]
    ffn()


def kernel(x, norm1_g, w_in, pool_w, pool_scale, conv_w, w_out, norm2_g, w_gate, w_up, w_down, normf_g):
    batch, seq, d = x.shape
    assert d == D_MODEL and seq % TM == 0
    n_tok = batch * seq
    bf16 = jnp.bfloat16
    n_groups = len(POOL_WINDOWS)

    def resident(shape):
        return pl.BlockSpec(shape, lambda i: (0,) * len(shape), pipeline_mode=pl.Buffered(1))

    hbm = pl.BlockSpec(memory_space=pl.ANY)
    tile = pl.BlockSpec((TM, D_MODEL), lambda i: (i, 0))
    out = pl.pallas_call(
        functools.partial(_block_kernel, seq // TM),
        grid=(n_tok // TM,),
        in_specs=[
            tile,
            resident((1, D_MODEL)),
            hbm,
            hbm,
            resident((1, POOL_WIDTH)),
            resident((CONV_K, CONV_WIDTH)),
            hbm,
            resident((1, D_MODEL)),
            hbm,
            hbm,
            hbm,
            resident((1, D_MODEL)),
        ],
        out_specs=tile,
        out_shape=jax.ShapeDtypeStruct((n_tok, D_MODEL), x.dtype),
        scratch_shapes=[
            pltpu.VMEM((D_MODEL, IN_COLS), bf16),
            pltpu.VMEM((MIX_WIDTH, D_MODEL), bf16),
            pltpu.VMEM((D_MODEL, D_FF), bf16),
            pltpu.VMEM((D_MODEL, D_FF), bf16),
            pltpu.VMEM((D_FF, D_MODEL), bf16),
            pltpu.VMEM((STAGE_SLOTS, STAGE_ROWS, STAGE_COLS), jnp.float32),
            pltpu.VMEM((n_groups, POOL_GROUP_DIM, POOL_GROUP_DIM), jnp.float32),
            pltpu.SemaphoreType.DMA((STAGE_SLOTS,)),
            pltpu.SemaphoreType.DMA(()),
            pltpu.VMEM((POOL_HALO + TM, POOL_WIDTH), jnp.float32),
            pltpu.VMEM((CONV_HALO + TM, CONV_WIDTH), jnp.float32),
            pltpu.VMEM((TM, CONV_WIDTH), jnp.float32),
            pltpu.VMEM((TM, MIX_WIDTH), bf16),
            pltpu.VMEM((TM, D_MODEL), bf16),
        ],
        compiler_params=pltpu.CompilerParams(
            dimension_semantics=("arbitrary",),
            vmem_limit_bytes=VMEM_LIMIT_BYTES,
        ),
        name="hybrid_block",
    )(
        x.reshape(n_tok, D_MODEL),
        norm1_g.reshape(1, D_MODEL),
        w_in,
        pool_w,
        pool_scale.reshape(1, POOL_WIDTH),
        conv_w,
        w_out,
        norm2_g.reshape(1, D_MODEL),
        w_gate,
        w_up,
        w_down,
        normf_g.reshape(1, D_MODEL),
    )
    return out.reshape(batch, seq, D_MODEL)
```

```python
import functools

import jax
import jax.numpy as jnp
from jax import lax
from jax.experimental import pallas as pl
from jax.experimental.pallas import tpu as pltpu

D_MODEL = 1024
POOL_WIDTH = 512
CONV_WIDTH = 512
MIX_WIDTH = POOL_WIDTH + CONV_WIDTH
POOL_WINDOWS = (2, 4, 8, 16)
POOL_GROUP_DIM = 128
CONV_K = 3
IN_COLS = POOL_WIDTH + 3 * CONV_WIDTH
D_FF = 2816
RMS_EPS = 1e-6

SUB = 256
SUBTILES = 4
TM = SUB * SUBTILES
POOL_HALO = 16
assert all(w & (w - 1) == 0 and w <= POOL_HALO for w in POOL_WINDOWS)
CONV_HALO = 8
FF_CHUNK = 256
FF_GROUP = 4
FINAL_ROWS = 256
STAGE_ROWS = 1024
STAGE_COLS = 512
STAGE_SLOTS = 4
VMEM_LIMIT_BYTES = 62 * 1024 * 1024


def _rmsnorm(x, g):
    ms = jnp.mean(x * x, axis=-1, keepdims=True)
    return (x * lax.rsqrt(ms + RMS_EPS)) * g


def _weight_chunks(w_in, w_out, w_gate, w_up, w_down, wb_in, wb_out, wb_gate, wb_up, wb_down):
    assert STAGE_COLS == POOL_WIDTH and STAGE_ROWS >= D_MODEL
    chunks = []
    for src, dst in ((w_in, wb_in), (w_out, wb_out), (w_gate, wb_gate), (w_up, wb_up), (w_down, wb_down)):
        n_rows, n_cols = src.shape
        for r0 in range(0, n_rows, STAGE_ROWS):
            rows = min(STAGE_ROWS, n_rows - r0)
            for c0 in range(0, n_cols, STAGE_COLS):
                cols = min(STAGE_COLS, n_cols - c0)
                idx = (pl.ds(r0, rows), pl.ds(c0, cols))
                chunks.append((src.at[idx], dst.at[idx], rows, cols, src is w_in and c0 == 0))
    return chunks


def _load_weights(w_in, pool_w, pool_scale_ref, w_out, w_gate, w_up, w_down,
                  wb_in, wb_out, wb_gate, wb_up, wb_down,
                  stage, pool_stage, sem, pool_sem):
    pool_copy = pltpu.make_async_copy(pool_w, pool_stage, pool_sem)
    pool_copy.start()
    chunks = _weight_chunks(w_in, w_out, w_gate, w_up, w_down, wb_in, wb_out, wb_gate, wb_up, wb_down)

    def copy(k):
        src, _, rows, cols, _ = chunks[k]
        slot = k % STAGE_SLOTS
        return pltpu.make_async_copy(src, stage.at[slot, pl.ds(0, rows), pl.ds(0, cols)], sem.at[slot])

    for k in range(min(STAGE_SLOTS - 1, len(chunks))):
        copy(k).start()
    for k, (_, dst, rows, cols, is_pool_section) in enumerate(chunks):
        if k + STAGE_SLOTS - 1 < len(chunks):
            copy(k + STAGE_SLOTS - 1).start()
        copy(k).wait()
        staged = stage.at[k % STAGE_SLOTS]
        if is_pool_section:
            pool_copy.wait()
            for g in range(len(POOL_WINDOWS)):
                gcols = slice(g * POOL_GROUP_DIM, (g + 1) * POOL_GROUP_DIM)
                group_map = pool_stage[g] * pool_scale_ref[:, gcols]
                dst[:, gcols] = jnp.dot(staged[0:rows, gcols], group_map,
                                        preferred_element_type=jnp.float32,
                                        precision=lax.Precision.HIGHEST).astype(jnp.bfloat16)
        else:
            dst[...] = staged[0:rows, 0:cols].astype(jnp.bfloat16)


def _block_kernel(tiles_per_seq,
                  x_hbm, g1_ref, w_in_hbm, pool_w_hbm, pool_scale_ref, conv_w_ref,
                  w_out_hbm, g2_ref, w_gate_hbm, w_up_hbm, w_down_hbm, gf_ref,
                  o_ref,
                  w_in_ref, w_out_ref, w_gate_ref, w_up_ref, w_down_ref,
                  stage, pool_stage, sem, pool_sem,
                  vbuf, ubuf, gbbuf, ymix, hn2buf, xbuf, xsem):
    i = pl.program_id(0)
    f32 = jnp.float32
    bf16 = jnp.bfloat16
    slot = i % 2

    def x_copy(step, into):
        return pltpu.make_async_copy(x_hbm.at[pl.ds(step * TM, TM), :], xbuf.at[into], xsem.at[into])

    @pl.when(i == 0)
    def _():
        x_copy(0, 0).start()
        _load_weights(w_in_hbm, pool_w_hbm, pool_scale_ref, w_out_hbm, w_gate_hbm, w_up_hbm, w_down_hbm,
                      w_in_ref, w_out_ref, w_gate_ref, w_up_ref, w_down_ref,
                      stage, pool_stage, sem, pool_sem)

    @pl.when(i + 1 < pl.num_programs(0))
    def _():
        x_copy(i + 1, 1 - slot).start()

    x_copy(i, slot).wait()
    x_ref = xbuf.at[slot]

    @pl.when(i % tiles_per_seq == 0)
    def _():
        vbuf[0:POOL_HALO, :] = jnp.zeros((POOL_HALO, POOL_WIDTH), f32)
        ubuf[0:CONV_HALO, :] = jnp.zeros((CONV_HALO, CONV_WIDTH), f32)

    def in_proj():
        pieces, scales = [], []
        for s in range(SUBTILES):
            x = x_ref[pl.ds(s * SUB, SUB), :]
            scales.append(lax.rsqrt(jnp.mean(x * x, axis=-1, keepdims=True) + RMS_EPS))
            pieces.append((x * g1_ref[...]).astype(bf16))
        hn = jnp.concatenate(pieces, axis=0)
        inv = jnp.concatenate(scales, axis=0)

        def section(k):
            cols = slice(k * POOL_WIDTH, (k + 1) * POOL_WIDTH)
            return jnp.dot(hn, w_in_ref[:, cols], preferred_element_type=f32)

        vbuf[pl.ds(POOL_HALO, TM), :] = section(0) * inv
        urows = pl.ds(CONV_HALO, TM)
        ubuf[urows, :] = section(2)
        ubuf[urows, :] = ubuf[urows, :] * section(3) * (inv * inv)
        gbbuf[...] = section(1) * inv

    def pool_mixer(s):
        rows = pl.ds(s * SUB, SUB)
        pos = (i % tiles_per_seq) * TM + s * SUB + lax.broadcasted_iota(jnp.int32, (SUB, 1), 0)
        n_seen = (pos + 1).astype(f32)
        for g, w in enumerate(POOL_WINDOWS):
            cols = slice(g * POOL_GROUP_DIM, (g + 1) * POOL_GROUP_DIM)
            ext = vbuf[pl.ds(s * SUB, POOL_HALO + SUB), cols]
            tot = ext
            for k in range(w.bit_length() - 1):
                tot = tot + pltpu.roll(tot, 1 << k, 0)
            inv_cnt = 1.0 / jnp.minimum(n_seen, float(w))
            ymix[rows, cols] = (tot[POOL_HALO:] * inv_cnt - ext[POOL_HALO:]).astype(bf16)

    def conv_mixer(s):
        rows = pl.ds(s * SUB, SUB)
        u0 = CONV_HALO + s * SUB
        conv = ubuf[pl.ds(u0, SUB), :] * conv_w_ref[CONV_K - 1:CONV_K, :]
        for k in range(CONV_K - 1):
            lag = CONV_K - 1 - k
            conv = conv + ubuf[pl.ds(u0 - lag, SUB), :] * conv_w_ref[k:k + 1, :]
        ymix[rows, POOL_WIDTH:] = (gbbuf[rows, :] * conv).astype(bf16)

    def out_proj(s):
        rows = pl.ds(s * SUB, SUB)
        o_ref[rows, :] = x_ref[rows, :] + jnp.dot(ymix[rows, :], w_out_ref[...],
                                                  preferred_element_type=f32)

    def ffn():
        for s in range(SUBTILES):
            rows = pl.ds(s * SUB, SUB)
            hn2buf[rows, :] = _rmsnorm(o_ref[rows, :], g2_ref[...]).astype(bf16)

        def activation(c):
            cols = slice(c * FF_CHUNK, (c + 1) * FF_CHUNK)
            gate = jnp.dot(hn2buf[...], w_gate_ref[:, cols], preferred_element_type=f32)
            up = jnp.dot(hn2buf[...], w_up_ref[:, cols], preferred_element_type=f32)
            return (gate * jax.nn.sigmoid(gate) * up).astype(bf16)

        n_chunks = D_FF // FF_CHUNK
        for c0 in range(0, n_chunks, FF_GROUP):
            c1 = min(c0 + FF_GROUP, n_chunks)
            a = jnp.concatenate([activation(c) for c in range(c0, c1)], axis=1)
            w_down_rows = w_down_ref[c0 * FF_CHUNK:c1 * FF_CHUNK, :]
            if c1 < n_chunks:
                o_ref[...] += jnp.dot(a, w_down_rows, preferred_element_type=f32)
            else:
                for r0 in range(0, TM, FINAL_ROWS):
                    rows = pl.ds(r0, FINAL_ROWS)
                    y = o_ref[rows, :] + jnp.dot(a[r0:r0 + FINAL_ROWS], w_down_rows, preferred_element_type=f32)
                    o_ref[rows, :] = _rmsnorm(y, gf_ref[...])

    in_proj()
    for s in range(SUBTILES):
        pool_mixer(s)
        conv_mixer(s)
        out_proj(s)
    vbuf[0:POOL_HALO, :] = vbuf[TM:TM + POOL_HALO, :]
    ubuf[0:CONV_HALO, :] = ubuf[TM:TM + CONV_HALO, :]
    ffn()


def kernel(x, norm1_g, w_in, pool_w, pool_scale, conv_w, w_out, norm2_g, w_gate, w_up, w_down, normf_g):
    batch, seq, d = x.shape
    assert d == D_MODEL and seq % TM == 0
    n_tok = batch * seq
    bf16 = jnp.bfloat16
    n_groups = len(POOL_WINDOWS)

    def resident(shape):
        return pl.BlockSpec(shape, lambda i: (0,) * len(shape), pipeline_mode=pl.Buffered(1))

    hbm = pl.BlockSpec(memory_space=pl.ANY)
    tile = pl.BlockSpec((TM, D_MODEL), lambda i: (i, 0))
    out = pl.pallas_call(
        functools.partial(_block_kernel, seq // TM),
        grid=(n_tok // TM,),
        in_specs=[
            hbm,
            resident((1, D_MODEL)),
            hbm,
            hbm,
            resident((1, POOL_WIDTH)),
            resident((CONV_K, CONV_WIDTH)),
            hbm,
            resident((1, D_MODEL)),
            hbm,
            hbm,
            hbm,
            resident((1, D_MODEL)),
        ],
        out_specs=tile,
        out_shape=jax.ShapeDtypeStruct((n_tok, D_MODEL), x.dtype),
        scratch_shapes=[
            pltpu.VMEM((D_MODEL, IN_COLS), bf16),
            pltpu.VMEM((MIX_WIDTH, D_MODEL), bf16),
            pltpu.VMEM((D_MODEL, D_FF), bf16),
            pltpu.VMEM((D_MODEL, D_FF), bf16),
            pltpu.VMEM((D_FF, D_MODEL), bf16),
            pltpu.VMEM((STAGE_SLOTS, STAGE_ROWS, STAGE_COLS), jnp.float32),
            pltpu.VMEM((n_groups, POOL_GROUP_DIM, POOL_GROUP_DIM), jnp.float32),
            pltpu.SemaphoreType.DMA((STAGE_SLOTS,)),
            pltpu.SemaphoreType.DMA(()),
            pltpu.VMEM((POOL_HALO + TM, POOL_WIDTH), jnp.float32),
            pltpu.VMEM((CONV_HALO + TM, CONV_WIDTH), jnp.float32),
            pltpu.VMEM((TM, CONV_WIDTH), jnp.float32),
            pltpu.VMEM((TM, MIX_WIDTH), bf16),
            pltpu.VMEM((TM, D_MODEL), bf16),
            pltpu.VMEM((2, TM, D_MODEL), jnp.float32),
            pltpu.SemaphoreType.DMA((2,)),
        ],
        compiler_params=pltpu.CompilerParams(
            dimension_semantics=("arbitrary",),
            vmem_limit_bytes=VMEM_LIMIT_BYTES,
        ),
        name="hybrid_block",
    )(
        x.reshape(n_tok, D_MODEL),
        norm1_g.reshape(1, D_MODEL),
        w_in,
        pool_w,
        pool_scale.reshape(1, POOL_WIDTH),
        conv_w,
        w_out,
        norm2_g.reshape(1, D_MODEL),
        w_gate,
        w_up,
        w_down,
        normf_g.reshape(1, D_MODEL),
    )
    return out.reshape(batch, seq, D_MODEL)
```

```python
import functools

import jax
import jax.numpy as jnp
from jax import lax
from jax.experimental import pallas as pl
from jax.experimental.pallas import tpu as pltpu

D_MODEL = 1024
POOL_WIDTH = 512
CONV_WIDTH = 512
MIX_WIDTH = POOL_WIDTH + CONV_WIDTH
POOL_WINDOWS = (2, 4, 8, 16)
POOL_GROUP_DIM = 128
CONV_K = 3
IN_COLS = POOL_WIDTH + 3 * CONV_WIDTH
D_FF = 2816
RMS_EPS = 1e-6

SUB = 256
SUBTILES = 4
TM = SUB * SUBTILES
POOL_HALO = 16
assert all(w & (w - 1) == 0 and w <= POOL_HALO for w in POOL_WINDOWS)
CONV_HALO = 8
FF_CHUNK = 256
FF_GROUP = 4
FINAL_ROWS = 256
STAGE_ROWS = 1024
STAGE_COLS = 512
STAGE_SLOTS = 4
VMEM_LIMIT_BYTES = 62 * 1024 * 1024


def _rmsnorm(x, g):
    ms = jnp.mean(x * x, axis=-1, keepdims=True)
    return (x * lax.rsqrt(ms + RMS_EPS)) * g


def _weight_chunks(w_in, w_out, w_gate, w_up, w_down, wb_in, wb_out, wb_gate, wb_up, wb_down):
    assert STAGE_COLS == POOL_WIDTH and STAGE_ROWS >= D_MODEL
    chunks = []
    for src, dst in ((w_in, wb_in), (w_out, wb_out), (w_gate, wb_gate), (w_up, wb_up), (w_down, wb_down)):
        n_rows, n_cols = src.shape
        for r0 in range(0, n_rows, STAGE_ROWS):
            rows = min(STAGE_ROWS, n_rows - r0)
            for c0 in range(0, n_cols, STAGE_COLS):
                cols = min(STAGE_COLS, n_cols - c0)
                idx = (pl.ds(r0, rows), pl.ds(c0, cols))
                chunks.append((src.at[idx], dst.at[idx], rows, cols, src is w_in and c0 == 0))
    return chunks


def _load_weights(w_in, pool_w, pool_scale_ref, w_out, w_gate, w_up, w_down,
                  wb_in, wb_out, wb_gate, wb_up, wb_down,
                  stage, pool_stage, sem, pool_sem):
    pool_copy = pltpu.make_async_copy(pool_w, pool_stage, pool_sem)
    pool_copy.start()
    chunks = _weight_chunks(w_in, w_out, w_gate, w_up, w_down, wb_in, wb_out, wb_gate, wb_up, wb_down)

    def copy(k):
        src, _, rows, cols, _ = chunks[k]
        slot = k % STAGE_SLOTS
        return pltpu.make_async_copy(src, stage.at[slot, pl.ds(0, rows), pl.ds(0, cols)], sem.at[slot])

    for k in range(min(STAGE_SLOTS - 1, len(chunks))):
        copy(k).start()
    for k, (_, dst, rows, cols, is_pool_section) in enumerate(chunks):
        if k + STAGE_SLOTS - 1 < len(chunks):
            copy(k + STAGE_SLOTS - 1).start()
        copy(k).wait()
        staged = stage.at[k % STAGE_SLOTS]
        if is_pool_section:
            pool_copy.wait()
            for g in range(len(POOL_WINDOWS)):
                gcols = slice(g * POOL_GROUP_DIM, (g + 1) * POOL_GROUP_DIM)
                group_map = pool_stage[g] * pool_scale_ref[:, gcols]
                dst[:, gcols] = jnp.dot(staged[0:rows, gcols], group_map,
                                        preferred_element_type=jnp.float32,
                                        precision=lax.Precision.HIGHEST).astype(jnp.bfloat16)
        else:
            dst[...] = staged[0:rows, 0:cols].astype(jnp.bfloat16)


def _block_kernel(tiles_per_seq,
                  x_ref, g1_ref, w_in_hbm, pool_w_hbm, pool_scale_ref, conv_w_ref,
                  w_out_hbm, g2_ref, w_gate_hbm, w_up_hbm, w_down_hbm, gf_ref,
                  o_ref,
                  w_in_ref, w_out_ref, w_gate_ref, w_up_ref, w_down_ref,
                  stage, pool_stage, sem, pool_sem,
                  vbuf, ubuf, gbbuf, ymix, hn2buf):
    i = pl.program_id(0)
    f32 = jnp.float32
    bf16 = jnp.bfloat16

    @pl.when(i == 0)
    def _():
        _load_weights(w_in_hbm, pool_w_hbm, pool_scale_ref, w_out_hbm, w_gate_hbm, w_up_hbm, w_down_hbm,
                      w_in_ref, w_out_ref, w_gate_ref, w_up_ref, w_down_ref,
                      stage, pool_stage, sem, pool_sem)

    @pl.when(i % tiles_per_seq == 0)
    def _():
        vbuf[0:POOL_HALO, :] = jnp.zeros((POOL_HALO, POOL_WIDTH), f32)
        ubuf[0:CONV_HALO, :] = jnp.zeros((CONV_HALO, CONV_WIDTH), f32)

    def in_proj():
        hn = jnp.concatenate(
            [_rmsnorm(x_ref[pl.ds(s * SUB, SUB), :], g1_ref[...]).astype(bf16) for s in range(SUBTILES)], axis=0)

        def section(k):
            cols = slice(k * POOL_WIDTH, (k + 1) * POOL_WIDTH)
            return jnp.dot(hn, w_in_ref[:, cols], preferred_element_type=f32)

        vbuf[pl.ds(POOL_HALO, TM), :] = section(0)
        urows = pl.ds(CONV_HALO, TM)
        ubuf[urows, :] = section(2)
        ubuf[urows, :] = ubuf[urows, :] * section(3)
        gbbuf[...] = section(1)

    def pool_mixer(s):
        rows = pl.ds(s * SUB, SUB)
        pos = (i % tiles_per_seq) * TM + s * SUB + lax.broadcasted_iota(jnp.int32, (SUB, 1), 0)
        n_seen = (pos + 1).astype(f32)
        for g, w in enumerate(POOL_WINDOWS):
            cols = slice(g * POOL_GROUP_DIM, (g + 1) * POOL_GROUP_DIM)
            ext = vbuf[pl.ds(s * SUB, POOL_HALO + SUB), cols]
            tot = ext
            for k in range(w.bit_length() - 1):
                tot = tot + pltpu.roll(tot, 1 << k, 0)
            inv_cnt = 1.0 / jnp.minimum(n_seen, float(w))
            ymix[rows, cols] = (tot[POOL_HALO:] * inv_cnt - ext[POOL_HALO:]).astype(bf16)

    def conv_mixer(s):
        rows = pl.ds(s * SUB, SUB)
        u0 = CONV_HALO + s * SUB
        conv = ubuf[pl.ds(u0, SUB), :] * conv_w_ref[CONV_K - 1:CONV_K, :]
        for k in range(CONV_K - 1):
            lag = CONV_K - 1 - k
            conv = conv + ubuf[pl.ds(u0 - lag, SUB), :] * conv_w_ref[k:k + 1, :]
        ymix[rows, POOL_WIDTH:] = (gbbuf[rows, :] * conv).astype(bf16)

    def out_proj(s):
        rows = pl.ds(s * SUB, SUB)
        o_ref[rows, :] = x_ref[rows, :] + jnp.dot(ymix[rows, :], w_out_ref[...],
                                                  preferred_element_type=f32)

    def ffn():
        for s in range(SUBTILES):
            rows = pl.ds(s * SUB, SUB)
            hn2buf[rows, :] = _rmsnorm(o_ref[rows, :], g2_ref[...]).astype(bf16)

        def activation(c):
            cols = slice(c * FF_CHUNK, (c + 1) * FF_CHUNK)
            gate = jnp.dot(hn2buf[...], w_gate_ref[:, cols], preferred_element_type=f32)
            up = jnp.dot(hn2buf[...], w_up_ref[:, cols], preferred_element_type=f32)
            return (gate * (0.5 * jnp.tanh(0.5 * gate) + 0.5) * up).astype(bf16)

        n_chunks = D_FF // FF_CHUNK
        for c0 in range(0, n_chunks, FF_GROUP):
            c1 = min(c0 + FF_GROUP, n_chunks)
            a = jnp.concatenate([activation(c) for c in range(c0, c1)], axis=1)
            w_down_rows = w_down_ref[c0 * FF_CHUNK:c1 * FF_CHUNK, :]
            if c1 < n_chunks:
                o_ref[...] += jnp.dot(a, w_down_rows, preferred_element_type=f32)
            else:
                for r0 in range(0, TM, FINAL_ROWS):
                    rows = pl.ds(r0, FINAL_ROWS)
                    y = o_ref[rows, :] + jnp.dot(a[r0:r0 + FINAL_ROWS], w_down_rows, preferred_element_type=f32)
                    o_ref[rows, :] = _rmsnorm(y, gf_ref[...])

    in_proj()
    for s in range(SUBTILES):
        pool_mixer(s)
        conv_mixer(s)
        out_proj(s)
    vbuf[0:POOL_HALO, :] = vbuf[TM:TM + POOL_HALO, :]
    ubuf[0:CONV_HALO, :] = ubuf[TM:TM + CONV_HALO, :]
    ffn()


def kernel(x, norm1_g, w_in, pool_w, pool_scale, conv_w, w_out, norm2_g, w_gate, w_up, w_down, normf_g):
    batch, seq, d = x.shape
    assert d == D_MODEL and seq % TM == 0
    n_tok = batch * seq
    bf16 = jnp.bfloat16
    n_groups = len(POOL_WINDOWS)

    def resident(shape):
        return pl.BlockSpec(shape, lambda i: (0,) * len(shape), pipeline_mode=pl.Buffered(1))

    hbm = pl.BlockSpec(memory_space=pl.ANY)
    tile = pl.BlockSpec((TM, D_MODEL), lambda i: (i, 0))
    out = pl.pallas_call(
        functools.partial(_block_kernel, seq // TM),
        grid=(n_tok // TM,),
        in_specs=[
            tile,
            resident((1, D_MODEL)),
            hbm,
            hbm,
            resident((1, POOL_WIDTH)),
            resident((CONV_K, CONV_WIDTH)),
            hbm,
            resident((1, D_MODEL)),
            hbm,
            hbm,
            hbm,
            resident((1, D_MODEL)),
        ],
        out_specs=tile,
        out_shape=jax.ShapeDtypeStruct((n_tok, D_MODEL), x.dtype),
        scratch_shapes=[
            pltpu.VMEM((D_MODEL, IN_COLS), bf16),
            pltpu.VMEM((MIX_WIDTH, D_MODEL), bf16),
            pltpu.VMEM((D_MODEL, D_FF), bf16),
            pltpu.VMEM((D_MODEL, D_FF), bf16),
            pltpu.VMEM((D_FF, D_MODEL), bf16),
            pltpu.VMEM((STAGE_SLOTS, STAGE_ROWS, STAGE_COLS), jnp.float32),
            pltpu.VMEM((n_groups, POOL_GROUP_DIM, POOL_GROUP_DIM), jnp.float32),
            pltpu.SemaphoreType.DMA((STAGE_SLOTS,)),
            pltpu.SemaphoreType.DMA(()),
            pltpu.VMEM((POOL_HALO + TM, POOL_WIDTH), jnp.float32),
            pltpu.VMEM((CONV_HALO + TM, CONV_WIDTH), jnp.float32),
            pltpu.VMEM((TM, CONV_WIDTH), jnp.float32),
            pltpu.VMEM((TM, MIX_WIDTH), bf16),
            pltpu.VMEM((TM, D_MODEL), bf16),
        ],
        compiler_params=pltpu.CompilerParams(
            dimension_semantics=("arbitrary",),
            vmem_limit_bytes=VMEM_LIMIT_BYTES,
        ),
        name="hybrid_block",
    )(
        x.reshape(n_tok, D_MODEL),
        norm1_g.reshape(1, D_MODEL),
        w_in,
        pool_w,
        pool_scale.reshape(1, POOL_WIDTH),
        conv_w,
        w_out,
        norm2_g.reshape(1, D_MODEL),
        w_gate,
        w_up,
        w_down,
        normf_g.reshape(1, D_MODEL),
    )
    return out.reshape(batch, seq, D_MODEL)
```

```python
import functools

import jax
import jax.numpy as jnp
from jax import lax
from jax.experimental import pallas as pl
from jax.experimental.pallas import tpu as pltpu

D_MODEL = 1024
POOL_WIDTH = 512
CONV_WIDTH = 512
MIX_WIDTH = POOL_WIDTH + CONV_WIDTH
POOL_WINDOWS = (2, 4, 8, 16)
POOL_GROUP_DIM = 128
CONV_K = 3
IN_COLS = POOL_WIDTH + 3 * CONV_WIDTH
D_FF = 2816
RMS_EPS = 1e-6

SUB = 512
SUBTILES = 2
TM = SUB * SUBTILES
POOL_HALO = 16
assert all(w & (w - 1) == 0 and w <= POOL_HALO for w in POOL_WINDOWS)
CONV_HALO = 8
FF_CHUNK = 256
FF_GROUP = 4
FINAL_ROWS = 256
STAGE_ROWS = 1024
STAGE_COLS = 512
STAGE_SLOTS = 4
VMEM_LIMIT_BYTES = 62 * 1024 * 1024


def _rmsnorm(x, g):
    ms = jnp.mean(x * x, axis=-1, keepdims=True)
    return (x * lax.rsqrt(ms + RMS_EPS)) * g


def _weight_chunks(w_in, w_out, w_gate, w_up, w_down, wb_in, wb_out, wb_gate, wb_up, wb_down):
    assert STAGE_COLS == POOL_WIDTH and STAGE_ROWS >= D_MODEL
    chunks = []
    for src, dst in ((w_in, wb_in), (w_out, wb_out), (w_gate, wb_gate), (w_up, wb_up), (w_down, wb_down)):
        n_rows, n_cols = src.shape
        for r0 in range(0, n_rows, STAGE_ROWS):
            rows = min(STAGE_ROWS, n_rows - r0)
            for c0 in range(0, n_cols, STAGE_COLS):
                cols = min(STAGE_COLS, n_cols - c0)
                idx = (pl.ds(r0, rows), pl.ds(c0, cols))
                chunks.append((src.at[idx], dst.at[idx], rows, cols, src is w_in and c0 == 0))
    return chunks


def _load_weights(w_in, pool_w, pool_scale_ref, w_out, w_gate, w_up, w_down,
                  wb_in, wb_out, wb_gate, wb_up, wb_down,
                  stage, pool_stage, sem, pool_sem):
    pool_copy = pltpu.make_async_copy(pool_w, pool_stage, pool_sem)
    pool_copy.start()
    chunks = _weight_chunks(w_in, w_out, w_gate, w_up, w_down, wb_in, wb_out, wb_gate, wb_up, wb_down)

    def copy(k):
        src, _, rows, cols, _ = chunks[k]
        slot = k % STAGE_SLOTS
        return pltpu.make_async_copy(src, stage.at[slot, pl.ds(0, rows), pl.ds(0, cols)], sem.at[slot])

    for k in range(min(STAGE_SLOTS - 1, len(chunks))):
        copy(k).start()
    for k, (_, dst, rows, cols, is_pool_section) in enumerate(chunks):
        if k + STAGE_SLOTS - 1 < len(chunks):
            copy(k + STAGE_SLOTS - 1).start()
        copy(k).wait()
        staged = stage.at[k % STAGE_SLOTS]
        if is_pool_section:
            pool_copy.wait()
            for g in range(len(POOL_WINDOWS)):
                gcols = slice(g * POOL_GROUP_DIM, (g + 1) * POOL_GROUP_DIM)
                group_map = pool_stage[g] * pool_scale_ref[:, gcols]
                dst[:, gcols] = jnp.dot(staged[0:rows, gcols], group_map,
                                        preferred_element_type=jnp.float32,
                                        precision=lax.Precision.HIGHEST).astype(jnp.bfloat16)
        else:
            dst[...] = staged[0:rows, 0:cols].astype(jnp.bfloat16)


def _block_kernel(tiles_per_seq,
                  x_ref, g1_ref, w_in_hbm, pool_w_hbm, pool_scale_ref, conv_w_ref,
                  w_out_hbm, g2_ref, w_gate_hbm, w_up_hbm, w_down_hbm, gf_ref,
                  o_ref,
                  w_in_ref, w_out_ref, w_gate_ref, w_up_ref, w_down_ref,
                  stage, pool_stage, sem, pool_sem,
                  vbuf, ubuf, gbbuf, ymix, hn2buf):
    i = pl.program_id(0)
    f32 = jnp.float32
    bf16 = jnp.bfloat16

    @pl.when(i == 0)
    def _():
        _load_weights(w_in_hbm, pool_w_hbm, pool_scale_ref, w_out_hbm, w_gate_hbm, w_up_hbm, w_down_hbm,
                      w_in_ref, w_out_ref, w_gate_ref, w_up_ref, w_down_ref,
                      stage, pool_stage, sem, pool_sem)

    @pl.when(i % tiles_per_seq == 0)
    def _():
        vbuf[0:POOL_HALO, :] = jnp.zeros((POOL_HALO, POOL_WIDTH), f32)
        ubuf[0:CONV_HALO, :] = jnp.zeros((CONV_HALO, CONV_WIDTH), f32)

    def in_proj():
        hn = jnp.concatenate(
            [_rmsnorm(x_ref[pl.ds(s * SUB, SUB), :], g1_ref[...]).astype(bf16) for s in range(SUBTILES)], axis=0)

        def section(k):
            cols = slice(k * POOL_WIDTH, (k + 1) * POOL_WIDTH)
            return jnp.dot(hn, w_in_ref[:, cols], preferred_element_type=f32)

        vbuf[pl.ds(POOL_HALO, TM), :] = section(0)
        urows = pl.ds(CONV_HALO, TM)
        ubuf[urows, :] = section(2)
        ubuf[urows, :] = ubuf[urows, :] * section(3)
        gbbuf[...] = section(1)

    def pool_mixer(s):
        rows = pl.ds(s * SUB, SUB)
        pos = (i % tiles_per_seq) * TM + s * SUB + lax.broadcasted_iota(jnp.int32, (SUB, 1), 0)
        n_seen = (pos + 1).astype(f32)
        for g, w in enumerate(POOL_WINDOWS):
            cols = slice(g * POOL_GROUP_DIM, (g + 1) * POOL_GROUP_DIM)
            ext = vbuf[pl.ds(s * SUB, POOL_HALO + SUB), cols]
            tot = ext
            for k in range(w.bit_length() - 1):
                tot = tot + pltpu.roll(tot, 1 << k, 0)
            inv_cnt = 1.0 / jnp.minimum(n_seen, float(w))
            ymix[rows, cols] = (tot[POOL_HALO:] * inv_cnt - ext[POOL_HALO:]).astype(bf16)

    def conv_mixer(s):
        rows = pl.ds(s * SUB, SUB)
        u0 = CONV_HALO + s * SUB
        conv = ubuf[pl.ds(u0, SUB), :] * conv_w_ref[CONV_K - 1:CONV_K, :]
        for k in range(CONV_K - 1):
            lag = CONV_K - 1 - k
            conv = conv + ubuf[pl.ds(u0 - lag, SUB), :] * conv_w_ref[k:k + 1, :]
        ymix[rows, POOL_WIDTH:] = (gbbuf[rows, :] * conv).astype(bf16)

    def out_proj(s):
        rows = pl.ds(s * SUB, SUB)
        o_ref[rows, :] = x_ref[rows, :] + jnp.dot(ymix[rows, :], w_out_ref[...],
                                                  preferred_element_type=f32)

    def ffn():
        for s in range(SUBTILES):
            rows = pl.ds(s * SUB, SUB)
            hn2buf[rows, :] = _rmsnorm(o_ref[rows, :], g2_ref[...]).astype(bf16)

        def activation(c):
            cols = slice(c * FF_CHUNK, (c + 1) * FF_CHUNK)
            gate = jnp.dot(hn2buf[...], w_gate_ref[:, cols], preferred_element_type=f32)
            up = jnp.dot(hn2buf[...], w_up_ref[:, cols], preferred_element_type=f32)
            return (gate * jax.nn.sigmoid(gate) * up).astype(bf16)

        n_chunks = D_FF // FF_CHUNK
        for c0 in range(0, n_chunks, FF_GROUP):
            c1 = min(c0 + FF_GROUP, n_chunks)
            a = jnp.concatenate([activation(c) for c in range(c0, c1)], axis=1)
            w_down_rows = w_down_ref[c0 * FF_CHUNK:c1 * FF_CHUNK, :]
            if c1 < n_chunks:
                o_ref[...] += jnp.dot(a, w_down_rows, preferred_element_type=f32)
            else:
                for r0 in range(0, TM, FINAL_ROWS):
                    rows = pl.ds(r0, FINAL_ROWS)
                    y = o_ref[rows, :] + jnp.dot(a[r0:r0 + FINAL_ROWS], w_down_rows, preferred_element_type=f32)
                    o_ref[rows, :] = _rmsnorm(y, gf_ref[...])

    in_proj()
    for s in range(SUBTILES):
        pool_mixer(s)
        conv_mixer(s)
        out_proj(s)
    vbuf[0:POOL_HALO, :] = vbuf[TM:TM + POOL_HALO, :]
    ubuf[0:CONV_HALO, :] = ubuf[TM:TM + CONV_HALO, :]
    ffn()


def kernel(x, norm1_g, w_in, pool_w, pool_scale, conv_w, w_out, norm2_g, w_gate, w_up, w_down, normf_g):
    batch, seq, d = x.shape
    assert d == D_MODEL and seq % TM == 0
    n_tok = batch * seq
    bf16 = jnp.bfloat16
    n_groups = len(POOL_WINDOWS)

    def resident(shape):
        return pl.BlockSpec(shape, lambda i: (0,) * len(shape), pipeline_mode=pl.Buffered(1))

    hbm = pl.BlockSpec(memory_space=pl.ANY)
    tile = pl.BlockSpec((TM, D_MODEL), lambda i: (i, 0))
    out = pl.pallas_call(
        functools.partial(_block_kernel, seq // TM),
        grid=(n_tok // TM,),
        in_specs=[
            tile,
            resident((1, D_MODEL)),
            hbm,
            hbm,
            resident((1, POOL_WIDTH)),
            resident((CONV_K, CONV_WIDTH)),
            hbm,
            resident((1, D_MODEL)),
            hbm,
            hbm,
            hbm,
            resident((1, D_MODEL)),
        ],
        out_specs=tile,
        out_shape=jax.ShapeDtypeStruct((n_tok, D_MODEL), x.dtype),
        scratch_shapes=[
            pltpu.VMEM((D_MODEL, IN_COLS), bf16),
            pltpu.VMEM((MIX_WIDTH, D_MODEL), bf16),
            pltpu.VMEM((D_MODEL, D_FF), bf16),
            pltpu.VMEM((D_MODEL, D_FF), bf16),
            pltpu.VMEM((D_FF, D_MODEL), bf16),
            pltpu.VMEM((STAGE_SLOTS, STAGE_ROWS, STAGE_COLS), jnp.float32),
            pltpu.VMEM((n_groups, POOL_GROUP_DIM, POOL_GROUP_DIM), jnp.float32),
            pltpu.SemaphoreType.DMA((STAGE_SLOTS,)),
            pltpu.SemaphoreType.DMA(()),
            pltpu.VMEM((POOL_HALO + TM, POOL_WIDTH), jnp.float32),
            pltpu.VMEM((CONV_HALO + TM, CONV_WIDTH), jnp.float32),
            pltpu.VMEM((TM, CONV_WIDTH), jnp.float32),
            pltpu.VMEM((TM, MIX_WIDTH), bf16),
            pltpu.VMEM((TM, D_MODEL), bf16),
        ],
        compiler_params=pltpu.CompilerParams(
            dimension_semantics=("arbitrary",),
            vmem_limit_bytes=VMEM_LIMIT_BYTES,
        ),
        name="hybrid_block",
    )(
        x.reshape(n_tok, D_MODEL),
        norm1_g.reshape(1, D_MODEL),
        w_in,
        pool_w,
        pool_scale.reshape(1, POOL_WIDTH),
        conv_w,
        w_out,
        norm2_g.reshape(1, D_MODEL),
        w_gate,
        w_up,
        w_down,
        normf_g.reshape(1, D_MODEL),
    )
    return out.reshape(batch, seq, D_MODEL)
```

```python
import functools

import jax
import jax.numpy as jnp
from jax import lax
from jax.experimental import pallas as pl
from jax.experimental.pallas import tpu as pltpu

D_MODEL = 1024
POOL_WIDTH = 512
CONV_WIDTH = 512
MIX_WIDTH = POOL_WIDTH + CONV_WIDTH
POOL_WINDOWS = (2, 4, 8, 16)
POOL_GROUP_DIM = 128
CONV_K = 3
IN_COLS = POOL_WIDTH + 3 * CONV_WIDTH
D_FF = 2816
RMS_EPS = 1e-6

SUB = 256
SUBTILES = 4
TM = SUB * SUBTILES
POOL_HALO = 16
assert all(w & (w - 1) == 0 and w <= POOL_HALO for w in POOL_WINDOWS)
CONV_HALO = 8
FF_CHUNK = 256
FF_GROUP = 4
FINAL_ROWS = 256
STAGE_ROWS = 1024
STAGE_COLS = 512
STAGE_SLOTS = 4
VMEM_LIMIT_BYTES = 62 * 1024 * 1024


def _rmsnorm(x, g):
    ms = jnp.mean(x * x, axis=-1, keepdims=True)
    return (x * lax.rsqrt(ms + RMS_EPS)) * g


def _weight_chunks(w_in, w_out, w_gate, w_up, w_down, wb_in, wb_out, wb_gate, wb_up, wb_down):
    assert STAGE_COLS == POOL_WIDTH and STAGE_ROWS >= D_MODEL
    chunks = []
    for src, dst in ((w_in, wb_in), (w_out, wb_out), (w_gate, wb_gate), (w_up, wb_up), (w_down, wb_down)):
        n_rows, n_cols = src.shape
        for r0 in range(0, n_rows, STAGE_ROWS):
            rows = min(STAGE_ROWS, n_rows - r0)
            for c0 in range(0, n_cols, STAGE_COLS):
                cols = min(STAGE_COLS, n_cols - c0)
                idx = (pl.ds(r0, rows), pl.ds(c0, cols))
                chunks.append((src.at[idx], dst.at[idx], rows, cols, src is w_in and c0 == 0))
    return chunks


def _load_weights(w_in, pool_w, pool_scale_ref, w_out, w_gate, w_up, w_down,
                  wb_in, wb_out, wb_gate, wb_up, wb_down,
                  stage, pool_stage, sem, pool_sem):
    pool_copy = pltpu.make_async_copy(pool_w, pool_stage, pool_sem)
    pool_copy.start()
    chunks = _weight_chunks(w_in, w_out, w_gate, w_up, w_down, wb_in, wb_out, wb_gate, wb_up, wb_down)

    def copy(k):
        src, _, rows, cols, _ = chunks[k]
        slot = k % STAGE_SLOTS
        return pltpu.make_async_copy(src, stage.at[slot, pl.ds(0, rows), pl.ds(0, cols)], sem.at[slot])

    for k in range(min(STAGE_SLOTS - 1, len(chunks))):
        copy(k).start()
    for k, (_, dst, rows, cols, is_pool_section) in enumerate(chunks):
        if k + STAGE_SLOTS - 1 < len(chunks):
            copy(k + STAGE_SLOTS - 1).start()
        copy(k).wait()
        staged = stage.at[k % STAGE_SLOTS]
        if is_pool_section:
            pool_copy.wait()
            for g in range(len(POOL_WINDOWS)):
                gcols = slice(g * POOL_GROUP_DIM, (g + 1) * POOL_GROUP_DIM)
                group_map = pool_stage[g] * pool_scale_ref[:, gcols]
                dst[:, gcols] = jnp.dot(staged[0:rows, gcols].astype(jnp.bfloat16), group_map.astype(jnp.bfloat16),
                                        preferred_element_type=jnp.float32).astype(jnp.bfloat16)
        else:
            dst[...] = staged[0:rows, 0:cols].astype(jnp.bfloat16)


def _block_kernel(tiles_per_seq,
                  x_ref, g1_ref, w_in_hbm, pool_w_hbm, pool_scale_ref, conv_w_ref,
                  w_out_hbm, g2_ref, w_gate_hbm, w_up_hbm, w_down_hbm, gf_ref,
                  o_ref,
                  w_in_ref, w_out_ref, w_gate_ref, w_up_ref, w_down_ref,
                  stage, pool_stage, sem, pool_sem,
                  vbuf, ubuf, gbbuf, ymix, hn2buf):
    i = pl.program_id(0)
    f32 = jnp.float32
    bf16 = jnp.bfloat16

    @pl.when(i == 0)
    def _():
        _load_weights(w_in_hbm, pool_w_hbm, pool_scale_ref, w_out_hbm, w_gate_hbm, w_up_hbm, w_down_hbm,
                      w_in_ref, w_out_ref, w_gate_ref, w_up_ref, w_down_ref,
                      stage, pool_stage, sem, pool_sem)

    @pl.when(i % tiles_per_seq == 0)
    def _():
        vbuf[0:POOL_HALO, :] = jnp.zeros((POOL_HALO, POOL_WIDTH), f32)
        ubuf[0:CONV_HALO, :] = jnp.zeros((CONV_HALO, CONV_WIDTH), f32)

    def in_proj():
        hn = jnp.concatenate(
            [_rmsnorm(x_ref[pl.ds(s * SUB, SUB), :], g1_ref[...]).astype(bf16) for s in range(SUBTILES)], axis=0)

        def section(k):
            cols = slice(k * POOL_WIDTH, (k + 1) * POOL_WIDTH)
            return jnp.dot(hn, w_in_ref[:, cols], preferred_element_type=f32)

        vbuf[pl.ds(POOL_HALO, TM), :] = section(0)
        urows = pl.ds(CONV_HALO, TM)
        ubuf[urows, :] = section(2)
        ubuf[urows, :] = ubuf[urows, :] * section(3)
        gbbuf[...] = section(1)

    def pool_mixer(s):
        rows = pl.ds(s * SUB, SUB)
        pos = (i % tiles_per_seq) * TM + s * SUB + lax.broadcasted_iota(jnp.int32, (SUB, 1), 0)
        n_seen = (pos + 1).astype(f32)
        for g, w in enumerate(POOL_WINDOWS):
            cols = slice(g * POOL_GROUP_DIM, (g + 1) * POOL_GROUP_DIM)
            ext = vbuf[pl.ds(s * SUB, POOL_HALO + SUB), cols]
            tot = ext
            for k in range(w.bit_length() - 1):
                tot = tot + pltpu.roll(tot, 1 << k, 0)
            inv_cnt = 1.0 / jnp.minimum(n_seen, float(w))
            ymix[rows, cols] = (tot[POOL_HALO:] * inv_cnt - ext[POOL_HALO:]).astype(bf16)

    def conv_mixer(s):
        rows = pl.ds(s * SUB, SUB)
        u0 = CONV_HALO + s * SUB
        conv = ubuf[pl.ds(u0, SUB), :] * conv_w_ref[CONV_K - 1:CONV_K, :]
        for k in range(CONV_K - 1):
            lag = CONV_K - 1 - k
            conv = conv + ubuf[pl.ds(u0 - lag, SUB), :] * conv_w_ref[k:k + 1, :]
        ymix[rows, POOL_WIDTH:] = (gbbuf[rows, :] * conv).astype(bf16)

    def out_proj(s):
        rows = pl.ds(s * SUB, SUB)
        o_ref[rows, :] = x_ref[rows, :] + jnp.dot(ymix[rows, :], w_out_ref[...],
                                                  preferred_element_type=f32)

    def ffn():
        for s in range(SUBTILES):
            rows = pl.ds(s * SUB, SUB)
            hn2buf[rows, :] = _rmsnorm(o_ref[rows, :], g2_ref[...]).astype(bf16)

        def activation(c):
            cols = slice(c * FF_CHUNK, (c + 1) * FF_CHUNK)
            gate = jnp.dot(hn2buf[...], w_gate_ref[:, cols], preferred_element_type=f32)
            up = jnp.dot(hn2buf[...], w_up_ref[:, cols], preferred_element_type=f32)
            return (gate * jax.nn.sigmoid(gate) * up).astype(bf16)

        n_chunks = D_FF // FF_CHUNK
        for c0 in range(0, n_chunks, FF_GROUP):
            c1 = min(c0 + FF_GROUP, n_chunks)
            a = jnp.concatenate([activation(c) for c in range(c0, c1)], axis=1)
            w_down_rows = w_down_ref[c0 * FF_CHUNK:c1 * FF_CHUNK, :]
            if c1 < n_chunks:
                o_ref[...] += jnp.dot(a, w_down_rows, preferred_element_type=f32)
            else:
                for r0 in range(0, TM, FINAL_ROWS):
                    rows = pl.ds(r0, FINAL_ROWS)
                    y = o_ref[rows, :] + jnp.dot(a[r0:r0 + FINAL_ROWS], w_down_rows, preferred_element_type=f32)
                    o_ref[rows, :] = _rmsnorm(y, gf_ref[...])

    in_proj()
    for s in range(SUBTILES):
        pool_mixer(s)
        conv_mixer(s)
        out_proj(s)
    vbuf[0:POOL_HALO, :] = vbuf[TM:TM + POOL_HALO, :]
    ubuf[0:CONV_HALO, :] = ubuf[TM:TM + CONV_HALO, :]
    ffn()


def kernel(x, norm1_g, w_in, pool_w, pool_scale, conv_w, w_out, norm2_g, w_gate, w_up, w_down, normf_g):
    batch, seq, d = x.shape
    assert d == D_MODEL and seq % TM == 0
    n_tok = batch * seq
    bf16 = jnp.bfloat16
    n_groups = len(POOL_WINDOWS)

    def resident(shape):
        return pl.BlockSpec(shape, lambda i: (0,) * len(shape), pipeline_mode=pl.Buffered(1))

    hbm = pl.BlockSpec(memory_space=pl.ANY)
    tile = pl.BlockSpec((TM, D_MODEL), lambda i: (i, 0))
    out = pl.pallas_call(
        functools.partial(_block_kernel, seq // TM),
        grid=(n_tok // TM,),
        in_specs=[
            tile,
            resident((1, D_MODEL)),
            hbm,
            hbm,
            resident((1, POOL_WIDTH)),
            resident((CONV_K, CONV_WIDTH)),
            hbm,
            resident((1, D_MODEL)),
            hbm,
            hbm,
            hbm,
            resident((1, D_MODEL)),
        ],
        out_specs=tile,
        out_shape=jax.ShapeDtypeStruct((n_tok, D_MODEL), x.dtype),
        scratch_shapes=[
            pltpu.VMEM((D_MODEL, IN_COLS), bf16),
            pltpu.VMEM((MIX_WIDTH, D_MODEL), bf16),
            pltpu.VMEM((D_MODEL, D_FF), bf16),
            pltpu.VMEM((D_MODEL, D_FF), bf16),
            pltpu.VMEM((D_FF, D_MODEL), bf16),
            pltpu.VMEM((STAGE_SLOTS, STAGE_ROWS, STAGE_COLS), jnp.float32),
            pltpu.VMEM((n_groups, POOL_GROUP_DIM, POOL_GROUP_DIM), jnp.float32),
            pltpu.SemaphoreType.DMA((STAGE_SLOTS,)),
            pltpu.SemaphoreType.DMA(()),
            pltpu.VMEM((POOL_HALO + TM, POOL_WIDTH), jnp.float32),
            pltpu.VMEM((CONV_HALO + TM, CONV_WIDTH), jnp.float32),
            pltpu.VMEM((TM, CONV_WIDTH), jnp.float32),
            pltpu.VMEM((TM, MIX_WIDTH), bf16),
            pltpu.VMEM((TM, D_MODEL), bf16),
        ],
        compiler_params=pltpu.CompilerParams(
            dimension_semantics=("arbitrary",),
            vmem_limit_bytes=VMEM_LIMIT_BYTES,
        ),
        name="hybrid_block",
    )(
        x.reshape(n_tok, D_MODEL),
        norm1_g.reshape(1, D_MODEL),
        w_in,
        pool_w,
        pool_scale.reshape(1, POOL_WIDTH),
        conv_w,
        w_out,
        norm2_g.reshape(1, D_MODEL),
        w_gate,
        w_up,
        w_down,
        normf_g.reshape(1, D_MODEL),
    )
    return out.reshape(batch, seq, D_MODEL)
```

```python
import functools

import jax
import jax.numpy as jnp
from jax import lax
from jax.experimental import pallas as pl
from jax.experimental.pallas import tpu as pltpu

D_MODEL = 1024
POOL_WIDTH = 512
CONV_WIDTH = 512
MIX_WIDTH = POOL_WIDTH + CONV_WIDTH
POOL_WINDOWS = (2, 4, 8, 16)
POOL_GROUP_DIM = 128
CONV_K = 3
IN_COLS = POOL_WIDTH + 3 * CONV_WIDTH
D_FF = 2816
RMS_EPS = 1e-6

SUB = 256
SUBTILES = 4
TM = SUB * SUBTILES
POOL_HALO = 16
assert all(w & (w - 1) == 0 and w <= POOL_HALO for w in POOL_WINDOWS)
CONV_HALO = 8
FF_CHUNK = 256
FF_GROUP = 6
FINAL_ROWS = 256
STAGE_ROWS = 1024
STAGE_COLS = 512
STAGE_SLOTS = 4
VMEM_LIMIT_BYTES = 62 * 1024 * 1024


def _rmsnorm(x, g):
    ms = jnp.mean(x * x, axis=-1, keepdims=True)
    return (x * lax.rsqrt(ms + RMS_EPS)) * g


def _weight_chunks(w_in, w_out, w_gate, w_up, w_down, wb_in, wb_out, wb_gate, wb_up, wb_down):
    assert STAGE_COLS == POOL_WIDTH and STAGE_ROWS >= D_MODEL
    chunks = []
    for src, dst in ((w_in, wb_in), (w_out, wb_out), (w_gate, wb_gate), (w_up, wb_up), (w_down, wb_down)):
        n_rows, n_cols = src.shape
        for r0 in range(0, n_rows, STAGE_ROWS):
            rows = min(STAGE_ROWS, n_rows - r0)
            for c0 in range(0, n_cols, STAGE_COLS):
                cols = min(STAGE_COLS, n_cols - c0)
                idx = (pl.ds(r0, rows), pl.ds(c0, cols))
                chunks.append((src.at[idx], dst.at[idx], rows, cols, src is w_in and c0 == 0))
    return chunks


def _load_weights(w_in, pool_w, pool_scale_ref, w_out, w_gate, w_up, w_down,
                  wb_in, wb_out, wb_gate, wb_up, wb_down,
                  stage, pool_stage, sem, pool_sem):
    pool_copy = pltpu.make_async_copy(pool_w, pool_stage, pool_sem)
    pool_copy.start()
    chunks = _weight_chunks(w_in, w_out, w_gate, w_up, w_down, wb_in, wb_out, wb_gate, wb_up, wb_down)

    def copy(k):
        src, _, rows, cols, _ = chunks[k]
        slot = k % STAGE_SLOTS
        return pltpu.make_async_copy(src, stage.at[slot, pl.ds(0, rows), pl.ds(0, cols)], sem.at[slot])

    for k in range(min(STAGE_SLOTS - 1, len(chunks))):
        copy(k).start()
    for k, (_, dst, rows, cols, is_pool_section) in enumerate(chunks):
        if k + STAGE_SLOTS - 1 < len(chunks):
            copy(k + STAGE_SLOTS - 1).start()
        copy(k).wait()
        staged = stage.at[k % STAGE_SLOTS]
        if is_pool_section:
            pool_copy.wait()
            for g in range(len(POOL_WINDOWS)):
                gcols = slice(g * POOL_GROUP_DIM, (g + 1) * POOL_GROUP_DIM)
                group_map = pool_stage[g] * pool_scale_ref[:, gcols]
                dst[:, gcols] = jnp.dot(staged[0:rows, gcols].astype(jnp.bfloat16), group_map.astype(jnp.bfloat16),
                                        preferred_element_type=jnp.float32).astype(jnp.bfloat16)
        else:
            dst[...] = staged[0:rows, 0:cols].astype(jnp.bfloat16)


def _block_kernel(tiles_per_seq,
                  x_ref, g1_ref, w_in_hbm, pool_w_hbm, pool_scale_ref, conv_w_ref,
                  w_out_hbm, g2_ref, w_gate_hbm, w_up_hbm, w_down_hbm, gf_ref,
                  o_ref,
                  w_in_ref, w_out_ref, w_gate_ref, w_up_ref, w_down_ref,
                  stage, pool_stage, sem, pool_sem,
                  vbuf, ubuf, gbbuf, ymix, hn2buf):
    i = pl.program_id(0)
    f32 = jnp.float32
    bf16 = jnp.bfloat16

    @pl.when(i == 0)
    def _():
        _load_weights(w_in_hbm, pool_w_hbm, pool_scale_ref, w_out_hbm, w_gate_hbm, w_up_hbm, w_down_hbm,
                      w_in_ref, w_out_ref, w_gate_ref, w_up_ref, w_down_ref,
                      stage, pool_stage, sem, pool_sem)

    @pl.when(i % tiles_per_seq == 0)
    def _():
        vbuf[0:POOL_HALO, :] = jnp.zeros((POOL_HALO, POOL_WIDTH), f32)
        ubuf[0:CONV_HALO, :] = jnp.zeros((CONV_HALO, CONV_WIDTH), f32)

    def in_proj():
        hn = jnp.concatenate(
            [_rmsnorm(x_ref[pl.ds(s * SUB, SUB), :], g1_ref[...]).astype(bf16) for s in range(SUBTILES)], axis=0)

        def section(k):
            cols = slice(k * POOL_WIDTH, (k + 1) * POOL_WIDTH)
            return jnp.dot(hn, w_in_ref[:, cols], preferred_element_type=f32)

        vbuf[pl.ds(POOL_HALO, TM), :] = section(0)
        urows = pl.ds(CONV_HALO, TM)
        ubuf[urows, :] = section(2)
        ubuf[urows, :] = ubuf[urows, :] * section(3)
        gbbuf[...] = section(1)

    def pool_mixer(s):
        rows = pl.ds(s * SUB, SUB)
        pos = (i % tiles_per_seq) * TM + s * SUB + lax.broadcasted_iota(jnp.int32, (SUB, 1), 0)
        n_seen = (pos + 1).astype(f32)
        for g, w in enumerate(POOL_WINDOWS):
            cols = slice(g * POOL_GROUP_DIM, (g + 1) * POOL_GROUP_DIM)
            ext = vbuf[pl.ds(s * SUB, POOL_HALO + SUB), cols]
            tot = ext
            for k in range(w.bit_length() - 1):
                tot = tot + pltpu.roll(tot, 1 << k, 0)
            inv_cnt = 1.0 / jnp.minimum(n_seen, float(w))
            ymix[rows, cols] = (tot[POOL_HALO:] * inv_cnt - ext[POOL_HALO:]).astype(bf16)

    def conv_mixer(s):
        rows = pl.ds(s * SUB, SUB)
        u0 = CONV_HALO + s * SUB
        conv = ubuf[pl.ds(u0, SUB), :] * conv_w_ref[CONV_K - 1:CONV_K, :]
        for k in range(CONV_K - 1):
            lag = CONV_K - 1 - k
            conv = conv + ubuf[pl.ds(u0 - lag, SUB), :] * conv_w_ref[k:k + 1, :]
        ymix[rows, POOL_WIDTH:] = (gbbuf[rows, :] * conv).astype(bf16)

    def out_proj(s):
        rows = pl.ds(s * SUB, SUB)
        o_ref[rows, :] = x_ref[rows, :] + jnp.dot(ymix[rows, :], w_out_ref[...],
                                                  preferred_element_type=f32)

    def ffn():
        for s in range(SUBTILES):
            rows = pl.ds(s * SUB, SUB)
            hn2buf[rows, :] = _rmsnorm(o_ref[rows, :], g2_ref[...]).astype(bf16)

        def activation(c):
            cols = slice(c * FF_CHUNK, (c + 1) * FF_CHUNK)
            gate = jnp.dot(hn2buf[...], w_gate_ref[:, cols], preferred_element_type=f32)
            up = jnp.dot(hn2buf[...], w_up_ref[:, cols], preferred_element_type=f32)
            return (gate * jax.nn.sigmoid(gate) * up).astype(bf16)

        n_chunks = D_FF // FF_CHUNK
        for c0 in range(0, n_chunks, FF_GROUP):
            c1 = min(c0 + FF_GROUP, n_chunks)
            a = jnp.concatenate([activation(c) for c in range(c0, c1)], axis=1)
            w_down_rows = w_down_ref[c0 * FF_CHUNK:c1 * FF_CHUNK, :]
            if c1 < n_chunks:
                o_ref[...] += jnp.dot(a, w_down_rows, preferred_element_type=f32)
            else:
                for r0 in range(0, TM, FINAL_ROWS):
                    rows = pl.ds(r0, FINAL_ROWS)
                    y = o_ref[rows, :] + jnp.dot(a[r0:r0 + FINAL_ROWS], w_down_rows, preferred_element_type=f32)
                    o_ref[rows, :] = _rmsnorm(y, gf_ref[...])

    in_proj()
    for s in range(SUBTILES):
        pool_mixer(s)
        conv_mixer(s)
        out_proj(s)
    vbuf[0:POOL_HALO, :] = vbuf[TM:TM + POOL_HALO, :]
    ubuf[0:CONV_HALO, :] = ubuf[TM:TM + CONV_HALO, :]
    ffn()


def kernel(x, norm1_g, w_in, pool_w, pool_scale, conv_w, w_out, norm2_g, w_gate, w_up, w_down, normf_g):
    batch, seq, d = x.shape
    assert d == D_MODEL and seq % TM == 0
    n_tok = batch * seq
    bf16 = jnp.bfloat16
    n_groups = len(POOL_WINDOWS)

    def resident(shape):
        return pl.BlockSpec(shape, lambda i: (0,) * len(shape), pipeline_mode=pl.Buffered(1))

    hbm = pl.BlockSpec(memory_space=pl.ANY)
    tile = pl.BlockSpec((TM, D_MODEL), lambda i: (i, 0))
    out = pl.pallas_call(
        functools.partial(_block_kernel, seq // TM),
        grid=(n_tok // TM,),
        in_specs=[
            tile,
            resident((1, D_MODEL)),
            hbm,
            hbm,
            resident((1, POOL_WIDTH)),
            resident((CONV_K, CONV_WIDTH)),
            hbm,
            resident((1, D_MODEL)),
            hbm,
            hbm,
            hbm,
            resident((1, D_MODEL)),
        ],
        out_specs=tile,
        out_shape=jax.ShapeDtypeStruct((n_tok, D_MODEL), x.dtype),
        scratch_shapes=[
            pltpu.VMEM((D_MODEL, IN_COLS), bf16),
            pltpu.VMEM((MIX_WIDTH, D_MODEL), bf16),
            pltpu.VMEM((D_MODEL, D_FF), bf16),
            pltpu.VMEM((D_MODEL, D_FF), bf16),
            pltpu.VMEM((D_FF, D_MODEL), bf16),
            pltpu.VMEM((STAGE_SLOTS, STAGE_ROWS, STAGE_COLS), jnp.float32),
            pltpu.VMEM((n_groups, POOL_GROUP_DIM, POOL_GROUP_DIM), jnp.float32),
            pltpu.SemaphoreType.DMA((STAGE_SLOTS,)),
            pltpu.SemaphoreType.DMA(()),
            pltpu.VMEM((POOL_HALO + TM, POOL_WIDTH), jnp.float32),
            pltpu.VMEM((CONV_HALO + TM, CONV_WIDTH), jnp.float32),
            pltpu.VMEM((TM, CONV_WIDTH), jnp.float32),
            pltpu.VMEM((TM, MIX_WIDTH), bf16),
            pltpu.VMEM((TM, D_MODEL), bf16),
        ],
        compiler_params=pltpu.CompilerParams(
            dimension_semantics=("arbitrary",),
            vmem_limit_bytes=VMEM_LIMIT_BYTES,
        ),
        name="hybrid_block",
    )(
        x.reshape(n_tok, D_MODEL),
        norm1_g.reshape(1, D_MODEL),
        w_in,
        pool_w,
        pool_scale.reshape(1, POOL_WIDTH),
        conv_w,
        w_out,
        norm2_g.reshape(1, D_MODEL),
        w_gate,
        w_up,
        w_down,
        normf_g.reshape(1, D_MODEL),
    )
    return out.reshape(batch, seq, D_MODEL)
```

```python
import functools

import jax
import jax.numpy as jnp
from jax import lax
from jax.experimental import pallas as pl
from jax.experimental.pallas import tpu as pltpu

D_MODEL = 1024
POOL_WIDTH = 512
CONV_WIDTH = 512
MIX_WIDTH = POOL_WIDTH + CONV_WIDTH
POOL_WINDOWS = (2, 4, 8, 16)
POOL_GROUP_DIM = 128
CONV_K = 3
IN_COLS = POOL_WIDTH + 3 * CONV_WIDTH
D_FF = 2816
RMS_EPS = 1e-6

SUB = 256
SUBTILES = 4
TM = SUB * SUBTILES
POOL_HALO = 16
assert all(w & (w - 1) == 0 and w <= POOL_HALO for w in POOL_WINDOWS)
CONV_HALO = 8
FF_CHUNK = 256
FF_GROUP = 4
FINAL_ROWS = 256
STAGE_ROWS = 1024
STAGE_COLS = 512
STAGE_SLOTS = 4
VMEM_LIMIT_BYTES = 62 * 1024 * 1024


def _rms_scale(x):
    ms = jnp.mean(x * x, axis=-1, keepdims=True)
    return x * lax.rsqrt(ms + RMS_EPS)


def _rmsnorm(x, g):
    return _rms_scale(x) * g


def _weight_chunks(w_in, w_out, w_gate, w_up, w_down, wb_in, wb_out, wb_gate, wb_up, wb_down, g1_col, g2_col):
    assert STAGE_COLS == POOL_WIDTH and STAGE_ROWS >= D_MODEL
    chunks = []
    for src, dst, gain in ((w_in, wb_in, g1_col), (w_out, wb_out, None), (w_gate, wb_gate, g2_col),
                           (w_up, wb_up, g2_col), (w_down, wb_down, None)):
        n_rows, n_cols = src.shape
        for r0 in range(0, n_rows, STAGE_ROWS):
            rows = min(STAGE_ROWS, n_rows - r0)
            for c0 in range(0, n_cols, STAGE_COLS):
                cols = min(STAGE_COLS, n_cols - c0)
                idx = (pl.ds(r0, rows), pl.ds(c0, cols))
                row_gain = None if gain is None else gain.at[pl.ds(r0, rows), :]
                chunks.append((src.at[idx], dst.at[idx], rows, cols, src is w_in and c0 == 0, row_gain))
    return chunks


def _load_weights(w_in, pool_w, pool_scale_ref, w_out, w_gate, w_up, w_down,
                  wb_in, wb_out, wb_gate, wb_up, wb_down, g1_col, g2_col,
                  stage, pool_stage, sem, pool_sem):
    pool_copy = pltpu.make_async_copy(pool_w, pool_stage, pool_sem)
    pool_copy.start()
    chunks = _weight_chunks(w_in, w_out, w_gate, w_up, w_down, wb_in, wb_out, wb_gate, wb_up, wb_down,
                            g1_col, g2_col)

    def copy(k):
        src, _, rows, cols, _, _ = chunks[k]
        slot = k % STAGE_SLOTS
        return pltpu.make_async_copy(src, stage.at[slot, pl.ds(0, rows), pl.ds(0, cols)], sem.at[slot])

    for k in range(min(STAGE_SLOTS - 1, len(chunks))):
        copy(k).start()
    for k, (_, dst, rows, cols, is_pool_section, row_gain) in enumerate(chunks):
        if k + STAGE_SLOTS - 1 < len(chunks):
            copy(k + STAGE_SLOTS - 1).start()
        copy(k).wait()
        staged = stage.at[k % STAGE_SLOTS]

        def rows_of(cols_slice):
            vals = staged[0:rows, cols_slice]
            return vals if row_gain is None else vals * row_gain[...]

        if is_pool_section:
            pool_copy.wait()
            for g in range(len(POOL_WINDOWS)):
                gcols = slice(g * POOL_GROUP_DIM, (g + 1) * POOL_GROUP_DIM)
                group_map = pool_stage[g] * pool_scale_ref[:, gcols]
                dst[:, gcols] = jnp.dot(rows_of(gcols).astype(jnp.bfloat16), group_map.astype(jnp.bfloat16),
                                        preferred_element_type=jnp.float32).astype(jnp.bfloat16)
        else:
            dst[...] = rows_of(slice(0, cols)).astype(jnp.bfloat16)


def _block_kernel(tiles_per_seq,
                  x_ref, g1_ref, w_in_hbm, pool_w_hbm, pool_scale_ref, conv_w_ref,
                  w_out_hbm, g2_ref, w_gate_hbm, w_up_hbm, w_down_hbm, gf_ref,
                  o_ref,
                  w_in_ref, w_out_ref, w_gate_ref, w_up_ref, w_down_ref,
                  stage, pool_stage, sem, pool_sem,
                  vbuf, ubuf, gbbuf, ymix, hn2buf):
    i = pl.program_id(0)
    f32 = jnp.float32
    bf16 = jnp.bfloat16

    @pl.when(i == 0)
    def _():
        _load_weights(w_in_hbm, pool_w_hbm, pool_scale_ref, w_out_hbm, w_gate_hbm, w_up_hbm, w_down_hbm,
                      w_in_ref, w_out_ref, w_gate_ref, w_up_ref, w_down_ref, g1_ref, g2_ref,
                      stage, pool_stage, sem, pool_sem)

    @pl.when(i % tiles_per_seq == 0)
    def _():
        vbuf[0:POOL_HALO, :] = jnp.zeros((POOL_HALO, POOL_WIDTH), f32)
        ubuf[0:CONV_HALO, :] = jnp.zeros((CONV_HALO, CONV_WIDTH), f32)

    def in_proj():
        hn = jnp.concatenate(
            [_rms_scale(x_ref[pl.ds(s * SUB, SUB), :]).astype(bf16) for s in range(SUBTILES)], axis=0)

        def section(k):
            cols = slice(k * POOL_WIDTH, (k + 1) * POOL_WIDTH)
            return jnp.dot(hn, w_in_ref[:, cols], preferred_element_type=f32)

        vbuf[pl.ds(POOL_HALO, TM), :] = section(0)
        urows = pl.ds(CONV_HALO, TM)
        ubuf[urows, :] = section(2)
        ubuf[urows, :] = ubuf[urows, :] * section(3)
        gbbuf[...] = section(1)

    def pool_mixer(s):
        rows = pl.ds(s * SUB, SUB)
        pos = (i % tiles_per_seq) * TM + s * SUB + lax.broadcasted_iota(jnp.int32, (SUB, 1), 0)
        n_seen = (pos + 1).astype(f32)
        for g, w in enumerate(POOL_WINDOWS):
            cols = slice(g * POOL_GROUP_DIM, (g + 1) * POOL_GROUP_DIM)
            ext = vbuf[pl.ds(s * SUB, POOL_HALO + SUB), cols]
            tot = ext
            for k in range(w.bit_length() - 1):
                tot = tot + pltpu.roll(tot, 1 << k, 0)
            inv_cnt = 1.0 / jnp.minimum(n_seen, float(w))
            ymix[rows, cols] = (tot[POOL_HALO:] * inv_cnt - ext[POOL_HALO:]).astype(bf16)

    def conv_mixer(s):
        rows = pl.ds(s * SUB, SUB)
        u0 = CONV_HALO + s * SUB
        conv = ubuf[pl.ds(u0, SUB), :] * conv_w_ref[CONV_K - 1:CONV_K, :]
        for k in range(CONV_K - 1):
            lag = CONV_K - 1 - k
            conv = conv + ubuf[pl.ds(u0 - lag, SUB), :] * conv_w_ref[k:k + 1, :]
        ymix[rows, POOL_WIDTH:] = (gbbuf[rows, :] * conv).astype(bf16)

    def out_proj(s):
        rows = pl.ds(s * SUB, SUB)
        o_ref[rows, :] = x_ref[rows, :] + jnp.dot(ymix[rows, :], w_out_ref[...],
                                                  preferred_element_type=f32)

    def ffn():
        for s in range(SUBTILES):
            rows = pl.ds(s * SUB, SUB)
            hn2buf[rows, :] = _rms_scale(o_ref[rows, :]).astype(bf16)

        def activation(c):
            cols = slice(c * FF_CHUNK, (c + 1) * FF_CHUNK)
            gate = jnp.dot(hn2buf[...], w_gate_ref[:, cols], preferred_element_type=f32)
            up = jnp.dot(hn2buf[...], w_up_ref[:, cols], preferred_element_type=f32)
            return (gate * jax.nn.sigmoid(gate) * up).astype(bf16)

        n_chunks = D_FF // FF_CHUNK
        for c0 in range(0, n_chunks, FF_GROUP):
            c1 = min(c0 + FF_GROUP, n_chunks)
            a = jnp.concatenate([activation(c) for c in range(c0, c1)], axis=1)
            w_down_rows = w_down_ref[c0 * FF_CHUNK:c1 * FF_CHUNK, :]
            if c1 < n_chunks:
                o_ref[...] += jnp.dot(a, w_down_rows, preferred_element_type=f32)
            else:
                for r0 in range(0, TM, FINAL_ROWS):
                    rows = pl.ds(r0, FINAL_ROWS)
                    y = o_ref[rows, :] + jnp.dot(a[r0:r0 + FINAL_ROWS], w_down_rows, preferred_element_type=f32)
                    o_ref[rows, :] = _rmsnorm(y, gf_ref[...])

    in_proj()
    for s in range(SUBTILES):
        pool_mixer(s)
        conv_mixer(s)
        out_proj(s)
    vbuf[0:POOL_HALO, :] = vbuf[TM:TM + POOL_HALO, :]
    ubuf[0:CONV_HALO, :] = ubuf[TM:TM + CONV_HALO, :]
    ffn()


def kernel(x, norm1_g, w_in, pool_w, pool_scale, conv_w, w_out, norm2_g, w_gate, w_up, w_down, normf_g):
    batch, seq, d = x.shape
    assert d == D_MODEL and seq % TM == 0
    n_tok = batch * seq
    bf16 = jnp.bfloat16
    n_groups = len(POOL_WINDOWS)

    def resident(shape):
        return pl.BlockSpec(shape, lambda i: (0,) * len(shape), pipeline_mode=pl.Buffered(1))

    hbm = pl.BlockSpec(memory_space=pl.ANY)
    tile = pl.BlockSpec((TM, D_MODEL), lambda i: (i, 0))
    out = pl.pallas_call(
        functools.partial(_block_kernel, seq // TM),
        grid=(n_tok // TM,),
        in_specs=[
            tile,
            resident((D_MODEL, 1)),
            hbm,
            hbm,
            resident((1, POOL_WIDTH)),
            resident((CONV_K, CONV_WIDTH)),
            hbm,
            resident((D_MODEL, 1)),
            hbm,
            hbm,
            hbm,
            resident((1, D_MODEL)),
        ],
        out_specs=tile,
        out_shape=jax.ShapeDtypeStruct((n_tok, D_MODEL), x.dtype),
        scratch_shapes=[
            pltpu.VMEM((D_MODEL, IN_COLS), bf16),
            pltpu.VMEM((MIX_WIDTH, D_MODEL), bf16),
            pltpu.VMEM((D_MODEL, D_FF), bf16),
            pltpu.VMEM((D_MODEL, D_FF), bf16),
            pltpu.VMEM((D_FF, D_MODEL), bf16),
            pltpu.VMEM((STAGE_SLOTS, STAGE_ROWS, STAGE_COLS), jnp.float32),
            pltpu.VMEM((n_groups, POOL_GROUP_DIM, POOL_GROUP_DIM), jnp.float32),
            pltpu.SemaphoreType.DMA((STAGE_SLOTS,)),
            pltpu.SemaphoreType.DMA(()),
            pltpu.VMEM((POOL_HALO + TM, POOL_WIDTH), jnp.float32),
            pltpu.VMEM((CONV_HALO + TM, CONV_WIDTH), jnp.float32),
            pltpu.VMEM((TM, CONV_WIDTH), jnp.float32),
            pltpu.VMEM((TM, MIX_WIDTH), bf16),
            pltpu.VMEM((TM, D_MODEL), bf16),
        ],
        compiler_params=pltpu.CompilerParams(
            dimension_semantics=("arbitrary",),
            vmem_limit_bytes=VMEM_LIMIT_BYTES,
        ),
        name="hybrid_block",
    )(
        x.reshape(n_tok, D_MODEL),
        norm1_g.reshape(D_MODEL, 1),
        w_in,
        pool_w,
        pool_scale.reshape(1, POOL_WIDTH),
        conv_w,
        w_out,
        norm2_g.reshape(D_MODEL, 1),
        w_gate,
        w_up,
        w_down,
        normf_g.reshape(1, D_MODEL),
    )
    return out.reshape(batch, seq, D_MODEL)
```
